```python
import math, functools
import jax, jax.numpy as jnp
from jax import lax
import numpy as np


D_MODEL = 4096
BATCH = 4
SEQ = 2048
DEPTH = 2
DEC_BATCH = 8
DEC_SEQ = 8
PAST_LEN = 16384
PAGE_SIZE = 128

HEAD_DIM = 128
V_DIM = 2 * HEAD_DIM
D_ATTN = D_MODEL // 2
N_HEADS = D_ATTN // V_DIM
D_SHORT = D_MODEL // 4
D_CONF = D_MODEL // 4
D_MIX = D_ATTN + D_SHORT + D_CONF
QK_COLS = N_HEADS * 2 * HEAD_DIM
D_IN = 2 * QK_COLS + N_HEADS * V_DIM + 3 * D_SHORT + 2 * D_CONF
SHORT_W = 3
CONF_W = 31
D_FF = 256 * ((8 * D_MODEL // 3 + 255) // 256)
ROPE_THETA = 10000.0
Q_BLOCK = 128
NORM_EPS = 1e-6
LN_EPS = 1e-5

kernel_name = 'hymba_diffattn_shortconv_conformer_macaron_step'


def rms_norm(x, g):
    xf = x.astype(jnp.float32)
    y = xf * lax.rsqrt(jnp.mean(xf * xf, axis=-1, keepdims=True) + NORM_EPS)
    return (y * g.astype(jnp.float32)).astype(x.dtype)


def layer_norm(x, g, b):
    xf = x.astype(jnp.float32)
    mu = jnp.mean(xf, axis=-1, keepdims=True)
    var = jnp.mean(jnp.square(xf - mu), axis=-1, keepdims=True)
    y = (xf - mu) * lax.rsqrt(var + LN_EPS) * g.astype(jnp.float32) + b.astype(jnp.float32)
    return y.astype(x.dtype)


def rope(x, positions):
    half = HEAD_DIM // 2
    inv_freq = jnp.power(ROPE_THETA, -jnp.arange(half, dtype=jnp.float32) * 2.0 / HEAD_DIM)
    ang = positions.astype(jnp.float32)[:, None] * inv_freq[None, :]
    cos = jnp.cos(ang)[:, None, None, :].astype(x.dtype)
    sin = jnp.sin(ang)[:, None, None, :].astype(x.dtype)
    x1, x2 = x[..., :half], x[..., half:]
    return jnp.concatenate([x1 * cos - x2 * sin, x2 * cos + x1 * sin], axis=-1)


def swiglu(x, w_gate, w_up, w_down):
    return (jax.nn.silu(x @ w_gate) * (x @ w_up)) @ w_down


def causal_dw_conv(ext, w):
    c = ext.shape[-1]
    return lax.conv_general_dilated(ext, w[:, None, :].astype(ext.dtype), window_strides=(1,),
                                    padding='VALID', dimension_numbers=('NWC', 'WIO', 'NWC'),
                                    feature_group_count=c)


def diff_lambda(lq1, lk1, lq2, lk2, lam_init):
    f = jnp.float32
    return (jnp.exp(jnp.sum(lq1.astype(f) * lk1.astype(f)))
            - jnp.exp(jnp.sum(lq2.astype(f) * lk2.astype(f))) + lam_init)


def diff_attn_prompt(q, k, v, lam):
    b, s = q.shape[0], q.shape[1]
    nb = s // Q_BLOCK
    scale = HEAD_DIM ** -0.5
    q_blocks = jnp.moveaxis(q.reshape(b, nb, Q_BLOCK, N_HEADS, 2, HEAD_DIM), 1, 0)
    k_pos = jnp.arange(s)

    def one_block(args):
        q_blk, blk = args
        scores = jnp.einsum('bqhmd,bkhmd->bhmqk', q_blk, k).astype(jnp.float32) * scale
        q_pos = blk * Q_BLOCK + jnp.arange(Q_BLOCK)
        scores = jnp.where(k_pos[None, :] <= q_pos[:, None], scores, -jnp.inf)
        probs = jax.nn.softmax(scores, axis=-1)
        weights = probs[:, :, 0] - lam * probs[:, :, 1]
        return jnp.einsum('bhqk,bkhd->bqhd', weights.astype(v.dtype), v)

    out = lax.map(one_block, (q_blocks, jnp.arange(nb)))
    return jnp.moveaxis(out, 0, 1).reshape(b, s, N_HEADS, V_DIM)


def diff_attn_sample(q, k, v, lam, k_past, v_past):
    t = q.shape[1]
    p_len = k_past.shape[1]
    scale = HEAD_DIM ** -0.5
    s_past = jnp.einsum('bqhmd,bkhmd->bhmqk', q, k_past).astype(jnp.float32) * scale
    s_new = jnp.einsum('bqhmd,bkhmd->bhmqk', q, k).astype(jnp.float32) * scale
    causal = jnp.arange(t)[None, :] <= jnp.arange(t)[:, None]
    scores = jnp.concatenate([s_past, jnp.where(causal, s_new, -jnp.inf)], axis=-1)
    probs = jax.nn.softmax(scores, axis=-1)
    weights = (probs[:, :, 0] - lam * probs[:, :, 1]).astype(v.dtype)
    return (jnp.einsum('bhqk,bkhd->bqhd', weights[..., :p_len], v_past)
            + jnp.einsum('bhqk,bkhd->bqhd', weights[..., p_len:], v))


def token_mix(h, positions, short_prev, conf_prev, attend, lp, lam, lam_init):
    bsz, s = h.shape[0], h.shape[1]
    z = h @ lp['w_in']
    sizes = [QK_COLS, QK_COLS, N_HEADS * V_DIM, D_SHORT, D_SHORT, D_SHORT, D_CONF, D_CONF]
    q, k, v, s_b, s_c, s_h, c_val, c_gate = jnp.split(z, np.cumsum(sizes)[:-1].tolist(), axis=-1)
    q = rope(rms_norm(q.reshape(bsz, s, N_HEADS, 2, HEAD_DIM), lp['q_norm']), positions)
    k = rope(rms_norm(k.reshape(bsz, s, N_HEADS, 2, HEAD_DIM), lp['k_norm']), positions)
    v = v.reshape(bsz, s, N_HEADS, V_DIM)
    a = attend(q, k, v, lam)
    a = rms_norm(a, lp['attn_out_norm']) * (1.0 - lam_init)
    short_ext = jnp.concatenate([short_prev, s_c * s_h], axis=1)
    y_short = s_b * causal_dw_conv(short_ext, lp['short_w'])
    conf_ext = jnp.concatenate([conf_prev, c_val * jax.nn.sigmoid(c_gate)], axis=1)
    y_conf = causal_dw_conv(conf_ext, lp['conf_w']) + lp['conf_b']
    y_conf = jax.nn.silu(layer_norm(y_conf, lp['conf_ln_g'], lp['conf_ln_b']))
    mix = jnp.concatenate([a.reshape(bsz, s, D_ATTN), y_short, y_conf], axis=-1)
    return (mix @ lp['w_out'], k.reshape(bsz, s, N_HEADS, 2 * HEAD_DIM), v,
            short_ext[:, -(SHORT_W - 1):], conf_ext[:, -(CONF_W - 1):])


def layer(x, positions, short_prev, conf_prev, attend, lp, lam, lam_init):
    x = x + 0.5 * swiglu(rms_norm(x, lp['norm_ffn1']), lp['ffn1_gate'], lp['ffn1_up'], lp['ffn1_down'])
    mix_out, k, v, s_state, c_state = token_mix(rms_norm(x, lp['norm_mix']), positions, short_prev,
                                                conf_prev, attend, lp, lam, lam_init)
    x = x + mix_out
    x = x + 0.5 * swiglu(rms_norm(x, lp['norm_ffn2']), lp['ffn2_gate'], lp['ffn2_up'], lp['ffn2_down'])
    return x, k, v, s_state, c_state


def setup_inputs(seed: int = 0) -> dict:
    key = jax.random.key(seed)
    ks = jax.random.split(key, 32)
    f32 = jnp.float32

    def nrm(k, shape, scale):
        return jax.random.normal(k, shape, f32) * scale

    n_pages = PAST_LEN // PAGE_SIZE
    n_used = DEC_BATCH * n_pages
    n_pool = n_used + max(1, n_used // 4)
    page_table = jax.random.permutation(ks[6], n_pool)[:n_used].reshape(DEC_BATCH, n_pages).astype(jnp.int32)
    return {
        'x_prompt': nrm(ks[0], (BATCH, SEQ, D_MODEL), 1.0),
        'x_sample': nrm(ks[1], (DEC_BATCH, DEC_SEQ, D_MODEL), 1.0),
        'cache_k': nrm(ks[2], (DEPTH, n_pool, PAGE_SIZE, N_HEADS, 2 * HEAD_DIM), 1.0),
        'cache_v': nrm(ks[3], (DEPTH, n_pool, PAGE_SIZE, N_HEADS, V_DIM), 1.0),
        'state_short': nrm(ks[4], (DEPTH, DEC_BATCH, SHORT_W - 1, D_SHORT), 1.0),
        'state_conf': nrm(ks[5], (DEPTH, DEC_BATCH, CONF_W - 1, D_CONF), 1.0),
        'page_table': page_table,
        'norm_ffn1': 1.0 + nrm(ks[7], (DEPTH, D_MODEL), 0.01),
        'ffn1_w_gate': nrm(ks[8], (DEPTH, D_MODEL, D_FF), D_MODEL ** -0.5),
        'ffn1_w_up': nrm(ks[9], (DEPTH, D_MODEL, D_FF), D_MODEL ** -0.5),
        'ffn1_w_down': nrm(ks[10], (DEPTH, D_FF, D_MODEL), D_FF ** -0.5),
        'norm_mix': 1.0 + nrm(ks[11], (DEPTH, D_MODEL), 0.01),
        'w_in': nrm(ks[12], (DEPTH, D_MODEL, D_IN), D_MODEL ** -0.5),
        'q_norm': 1.0 + nrm(ks[13], (DEPTH, HEAD_DIM), 0.01),
        'k_norm': 1.0 + nrm(ks[14], (DEPTH, HEAD_DIM), 0.01),
        'lambda_q1': nrm(ks[15], (DEPTH, HEAD_DIM), 0.1),
        'lambda_k1': nrm(ks[16], (DEPTH, HEAD_DIM), 0.1),
        'lambda_q2': nrm(ks[17], (DEPTH, HEAD_DIM), 0.1),
        'lambda_k2': nrm(ks[18], (DEPTH, HEAD_DIM), 0.1),
        'attn_out_norm': 1.0 + nrm(ks[19], (DEPTH, V_DIM), 0.01),
        'short_w': nrm(ks[20], (DEPTH, SHORT_W, D_SHORT), SHORT_W ** -0.5),
        'conf_w': nrm(ks[21], (DEPTH, CONF_W, D_CONF), CONF_W ** -0.5),
        'conf_b': nrm(ks[22], (DEPTH, D_CONF), 0.01),
        'conf_ln_g': 1.0 + nrm(ks[23], (DEPTH, D_CONF), 0.01),
        'conf_ln_b': nrm(ks[24], (DEPTH, D_CONF), 0.01),
        'w_out': nrm(ks[25], (DEPTH, D_MIX, D_MODEL), D_MIX ** -0.5),
        'norm_ffn2': 1.0 + nrm(ks[26], (DEPTH, D_MODEL), 0.01),
        'ffn2_w_gate': nrm(ks[27], (DEPTH, D_MODEL, D_FF), D_MODEL ** -0.5),
        'ffn2_w_up': nrm(ks[28], (DEPTH, D_MODEL, D_FF), D_MODEL ** -0.5),
        'ffn2_w_down': nrm(ks[29], (DEPTH, D_FF, D_MODEL), D_FF ** -0.5),
    }


def reference(x_prompt, x_sample, cache_k, cache_v, state_short, state_conf, page_table,
              norm_ffn1, ffn1_w_gate, ffn1_w_up, ffn1_w_down, norm_mix, w_in, q_norm, k_norm,
              lambda_q1, lambda_k1, lambda_q2, lambda_k2, attn_out_norm, short_w, conf_w, conf_b,
              conf_ln_g, conf_ln_b, w_out, norm_ffn2, ffn2_w_gate, ffn2_w_up, ffn2_w_down):
    bsz, seq = x_prompt.shape[0], x_prompt.shape[1]
    dec_b, dec_s = x_sample.shape[0], x_sample.shape[1]
    past_len = page_table.shape[1] * cache_k.shape[2]
    pos_prompt = jnp.arange(seq)
    pos_sample = past_len + jnp.arange(dec_s)
    short_zero = jnp.zeros((bsz, SHORT_W - 1, D_SHORT), x_prompt.dtype)
    conf_zero = jnp.zeros((bsz, CONF_W - 1, D_CONF), x_prompt.dtype)

    y_p, y_s = x_prompt, x_sample
    kp_l, vp_l, sp_l, cp_l, ks_l, vs_l, ss_l, cs_l = [], [], [], [], [], [], [], []
    for l in range(DEPTH):
        lp = dict(norm_ffn1=norm_ffn1[l], ffn1_gate=ffn1_w_gate[l], ffn1_up=ffn1_w_up[l],
                  ffn1_down=ffn1_w_down[l], norm_mix=norm_mix[l], w_in=w_in[l], q_norm=q_norm[l],
                  k_norm=k_norm[l], attn_out_norm=attn_out_norm[l], short_w=short_w[l],
                  conf_w=conf_w[l], conf_b=conf_b[l], conf_ln_g=conf_ln_g[l], conf_ln_b=conf_ln_b[l],
                  w_out=w_out[l], norm_ffn2=norm_ffn2[l], ffn2_gate=ffn2_w_gate[l],
                  ffn2_up=ffn2_w_up[l], ffn2_down=ffn2_w_down[l])
        lam_init = 0.8 - 0.6 * math.exp(-0.3 * l)
        lam = diff_lambda(lambda_q1[l], lambda_k1[l], lambda_q2[l], lambda_k2[l], lam_init)

        y_p, k_p, v_p, s_p, c_p = layer(y_p, pos_prompt, short_zero, conf_zero, diff_attn_prompt,
                                        lp, lam, lam_init)
        k_past = cache_k[l][page_table].reshape(dec_b, past_len, N_HEADS, 2, HEAD_DIM)
        v_past = cache_v[l][page_table].reshape(dec_b, past_len, N_HEADS, V_DIM)
        attend_s = functools.partial(diff_attn_sample, k_past=k_past, v_past=v_past)
        y_s, k_s, v_s, s_s, c_s = layer(y_s, pos_sample, state_short[l], state_conf[l], attend_s,
                                        lp, lam, lam_init)
        kp_l.append(k_p); vp_l.append(v_p); sp_l.append(s_p); cp_l.append(c_p)
        ks_l.append(k_s); vs_l.append(v_s); ss_l.append(s_s); cs_l.append(c_s)

    k_prompt, v_prompt = jnp.stack(kp_l), jnp.stack(vp_l)
    short_prompt, conf_prompt = jnp.stack(sp_l), jnp.stack(cp_l)
    k_sample, v_sample = jnp.stack(ks_l), jnp.stack(vs_l)
    short_sample, conf_sample = jnp.stack(ss_l), jnp.stack(cs_l)
    return (y_p, y_s, k_prompt, v_prompt, short_prompt, conf_prompt, k_sample, v_sample, short_sample, conf_sample)
```

```python
import functools
import math

import jax
import jax.numpy as jnp
from jax import lax
from jax.experimental import pallas as pl
from jax.experimental.pallas import tpu as pltpu

F32 = jnp.float32
BF16 = jnp.bfloat16

HEAD_DIM = 128
V_DIM = 2 * HEAD_DIM
SHORT_W = 3
CONF_W = 31
ROPE_THETA = 10000.0
NORM_EPS = 1e-6
LN_EPS = 1e-5

LANES = 128
SUBLANES_F32 = 8
SUBLANES_BF16 = 16
VMEM_LIMIT_BYTES = 56 * 1024 * 1024

SHORT_PAD = 8
CONF_PAD = 32


def _pick_tile(n, cap, align):
    best = None
    for d in range(align, min(n, cap) + 1, align):
        if n % d == 0:
            best = d
    if best is None:
        raise ValueError(f"no tile for n={n} cap={cap} align={align}")
    return best


def _params(*sem):
    return pltpu.CompilerParams(dimension_semantics=sem, vmem_limit_bytes=VMEM_LIMIT_BYTES)


def _rms(x, g):
    ms = jnp.mean(x * x, axis=-1, keepdims=True)
    return x * lax.rsqrt(ms + NORM_EPS) * g


def _norm_body(x_ref, g_ref, h_ref):
    h_ref[...] = _rms(x_ref[...], g_ref[...]).astype(BF16)


def _add_norm_body(x_ref, d_ref, g_ref, xo_ref, h_ref):
    x = x_ref[...] + d_ref[...]
    xo_ref[...] = x
    h_ref[...] = _rms(x, g_ref[...]).astype(BF16)


def _add_body(x_ref, d_ref, xo_ref):
    xo_ref[...] = x_ref[...] + d_ref[...]


def _row_spec(tr, d):
    return pl.BlockSpec((tr, d), lambda i: (i, 0))


def _norm(x, g):
    m, d = x.shape
    tr = _pick_tile(m, 192, SUBLANES_BF16)
    return pl.pallas_call(
        _norm_body, grid=(m // tr,),
        in_specs=[_row_spec(tr, d), pl.BlockSpec((1, d), lambda i: (0, 0))],
        out_specs=_row_spec(tr, d),
        out_shape=jax.ShapeDtypeStruct((m, d), BF16),
        compiler_params=_params("parallel"), name="rmsnorm")(x, g)


def _add_norm(x, delta, g):
    m, d = x.shape
    tr = _pick_tile(m, 192, SUBLANES_BF16)
    return pl.pallas_call(
        _add_norm_body, grid=(m // tr,),
        in_specs=[_row_spec(tr, d), _row_spec(tr, d), pl.BlockSpec((1, d), lambda i: (0, 0))],
        out_specs=[_row_spec(tr, d), _row_spec(tr, d)],
        out_shape=[jax.ShapeDtypeStruct((m, d), F32), jax.ShapeDtypeStruct((m, d), BF16)],
        compiler_params=_params("parallel"), name="add_rmsnorm")(x, delta, g)


def _add(x, delta):
    m, d = x.shape
    tr = _pick_tile(m, 192, SUBLANES_F32)
    return pl.pallas_call(
        _add_body, grid=(m // tr,),
        in_specs=[_row_spec(tr, d), _row_spec(tr, d)],
        out_specs=_row_spec(tr, d),
        out_shape=jax.ShapeDtypeStruct((m, d), F32),
        compiler_params=_params("parallel"), name="residual_add")(x, delta)


def _ffn_body(h_ref, wg_ref, wu_ref, wd_ref, o_ref, *, n_chunk):
    f = pl.program_id(1)
    h = h_ref[...]
    g = jnp.dot(h, wg_ref[...], preferred_element_type=F32)
    u = jnp.dot(h, wu_ref[...], preferred_element_type=F32)
    a = (g * jax.nn.sigmoid(g) * u * 0.5).astype(BF16)
    d_model = o_ref.shape[1]
    for n0 in range(0, d_model, n_chunk):
        part = jnp.dot(a, wd_ref[:, n0:n0 + n_chunk], preferred_element_type=F32)

        @pl.when(f == 0)
        def _():
            o_ref[:, n0:n0 + n_chunk] = part

        @pl.when(f > 0)
        def _():
            o_ref[:, n0:n0 + n_chunk] += part


def _ffn(h, wg, wu, wd, layer):
    m, d = h.shape
    d_ff = wg.shape[2]
    tm = _pick_tile(m, 704, SUBLANES_BF16)
    tf = _pick_tile(d_ff, 256, LANES)
    n_chunk = _pick_tile(d, 512, LANES)
    return pl.pallas_call(
        functools.partial(_ffn_body, n_chunk=n_chunk),
        grid=(m // tm, d_ff // tf),
        in_specs=[pl.BlockSpec((tm, d), lambda i, f: (i, 0)),
                  pl.BlockSpec((None, d, tf), lambda i, f: (layer, 0, f)),
                  pl.BlockSpec((None, d, tf), lambda i, f: (layer, 0, f)),
                  pl.BlockSpec((None, tf, d), lambda i, f: (layer, f, 0))],
        out_specs=pl.BlockSpec((tm, d), lambda i, f: (i, 0)),
        out_shape=jax.ShapeDtypeStruct((m, d), F32),
        compiler_params=_params("parallel", "arbitrary"), name="swiglu_ffn")(h, wg, wu, wd)


def _matmul_body(a_ref, w_ref, o_ref):
    o_ref[...] = jnp.dot(a_ref[...], w_ref[...], preferred_element_type=F32)


def _matmul(a, w, layer, name):
    m, k = a.shape
    n = w.shape[2]
    tm = _pick_tile(m, 704, SUBLANES_BF16)
    tn = _pick_tile(n, 1024, LANES)
    return pl.pallas_call(
        _matmul_body, grid=(n // tn, m // tm),
        in_specs=[pl.BlockSpec((tm, k), lambda j, i: (i, 0)),
                  pl.BlockSpec((None, k, tn), lambda j, i: (layer, 0, j))],
        out_specs=pl.BlockSpec((tm, tn), lambda j, i: (i, j)),
        out_shape=jax.ShapeDtypeStruct((m, n), F32),
        compiler_params=_params("parallel", "parallel"), name=name)(a, w)


def _qk_body(z_ref, cos_ref, sin_ref, g_ref, o_ref, *, n_q_chunks):
    cos = cos_ref[...]
    sin = sin_ref[...]
    n_chunks = o_ref.shape[1] // HEAD_DIM
    for c in range(n_chunks):
        gi = 0 if c < n_q_chunks else 1
        y = _rms(z_ref[:, c * HEAD_DIM:(c + 1) * HEAD_DIM], g_ref[gi:gi + 1, :])
        o_ref[:, c * HEAD_DIM:(c + 1) * HEAD_DIM] = y * cos + pltpu.roll(y, HEAD_DIM // 2, 1) * sin


def _qk_prep(z, cos, sin, gains, qk_cols):
    m = z.shape[0]
    tr = _pick_tile(m, 344, SUBLANES_F32)
    width = 2 * qk_cols
    return pl.pallas_call(
        functools.partial(_qk_body, n_q_chunks=qk_cols // HEAD_DIM),
        grid=(m // tr,),
        in_specs=[pl.BlockSpec((tr, width), lambda i: (i, 0)),
                  pl.BlockSpec((tr, HEAD_DIM), lambda i: (i, 0)),
                  pl.BlockSpec((tr, HEAD_DIM), lambda i: (i, 0)),
                  pl.BlockSpec((2, HEAD_DIM), lambda i: (0, 0))],
        out_specs=pl.BlockSpec((tr, width), lambda i: (i, 0)),
        out_shape=jax.ShapeDtypeStruct((m, width), F32),
        compiler_params=_params("parallel"), name="qk_norm_rope")(z, cos, sin, gains)


def _lambda(lp, lam_init):
    s1 = jnp.sum(lp[0:1, :] * lp[1:2, :], axis=-1, keepdims=True)
    s2 = jnp.sum(lp[2:3, :] * lp[3:4, :], axis=-1, keepdims=True)
    return jnp.exp(s1) - jnp.exp(s2) + lam_init


def _attn_prompt_body(lp_ref, g_ref, q_ref, k_ref, v_ref, o_ref, kb_ref, vb_ref, *, tq, lam_init):
    seq = q_ref.shape[0]
    scale = HEAD_DIM ** -0.5
    lam = _lambda(lp_ref[...], lam_init)
    kb_ref[...] = k_ref[...].astype(BF16)
    vb_ref[...] = v_ref[...].astype(BF16)
    for qi in range(seq // tq):
        n = (qi + 1) * tq
        qb = q_ref[qi * tq:(qi + 1) * tq, :].astype(BF16)
        row = lax.broadcasted_iota(jnp.int32, (tq, n), 0) + qi * tq
        col = lax.broadcasted_iota(jnp.int32, (tq, n), 1)
        visible = col <= row
        probs = []
        for mp in range(2):
            lo, hi = mp * HEAD_DIM, (mp + 1) * HEAD_DIM
            s = lax.dot_general(qb[:, lo:hi], kb_ref[0:n, lo:hi], (((1,), (1,)), ((), ())),
                                preferred_element_type=F32) * scale
            s = jnp.where(visible, s, -jnp.inf)
            p = jnp.exp(s - jnp.max(s, axis=-1, keepdims=True))
            probs.append((p, 1.0 / jnp.sum(p, axis=-1, keepdims=True)))
        (p1, r1), (p2, r2) = probs
        w = (p1 * r1 - p2 * (lam * r2)).astype(BF16)
        o = jnp.dot(w, vb_ref[0:n, :], preferred_element_type=F32)
        o = _rms(o, g_ref[...]) * (1.0 - lam_init)
        o_ref[qi * tq:(qi + 1) * tq, :] = o.astype(BF16)


def _attn_prompt(qk, z, lam_params, g_out, batch, seq, n_heads, lam_init):
    k_blk0 = n_heads
    v_blk0 = 2 * n_heads
    tq = _pick_tile(seq, 256, SUBLANES_BF16)
    return pl.pallas_call(
        functools.partial(_attn_prompt_body, tq=tq, lam_init=lam_init),
        grid=(batch, n_heads),
        in_specs=[pl.BlockSpec((4, HEAD_DIM), lambda b, h: (0, 0)),
                  pl.BlockSpec((1, V_DIM), lambda b, h: (0, 0)),
                  pl.BlockSpec((seq, V_DIM), lambda b, h: (b, h)),
                  pl.BlockSpec((seq, V_DIM), lambda b, h: (b, k_blk0 + h)),
                  pl.BlockSpec((seq, V_DIM), lambda b, h: (b, v_blk0 + h))],
        out_specs=pl.BlockSpec((seq, V_DIM), lambda b, h: (b, h)),
        out_shape=jax.ShapeDtypeStruct((batch * seq, n_heads * V_DIM), BF16),
        scratch_shapes=[pltpu.VMEM((seq, V_DIM), BF16), pltpu.VMEM((seq, V_DIM), BF16)],
        compiler_params=_params("parallel", "parallel"), name="diff_attn_prompt")(
            lam_params, g_out, qk, qk, z)


def _diag_column(x, valid):
    n = x.shape[1]
    r = jnp.sum(jnp.where(valid, x, 0.0), axis=0, keepdims=True)
    eye = lax.broadcasted_iota(jnp.int32, (n, n), 0) == lax.broadcasted_iota(jnp.int32, (n, n), 1)
    return jnp.sum(jnp.where(eye, jnp.broadcast_to(r, (n, n)), 0.0), axis=1, keepdims=True)


def _attn_sample_body(pt_ref, lp_ref, g_ref, qt_ref, *rest, n_pages_step, n_heads, t_new, lam_init):
    k_refs = rest[:n_pages_step]
    v_refs = rest[n_pages_step:2 * n_pages_step]
    kn_ref, vn_ref, o_ref, m_ref, l_ref, acc_ref = rest[2 * n_pages_step:]
    j = pl.program_id(1)
    scale = HEAD_DIM ** -0.5
    qt = qt_ref[...]
    n_cols = qt.shape[1]
    col_head = lax.broadcasted_iota(jnp.int32, (n_heads, n_cols), 1) // (2 * t_new)
    valid = col_head == lax.broadcasted_iota(jnp.int32, (n_heads, n_cols), 0)

    @pl.when(j == 0)
    def _():
        m_ref[...] = jnp.full(m_ref.shape, -jnp.inf, F32)
        l_ref[...] = jnp.zeros(l_ref.shape, F32)
        acc_ref[...] = jnp.zeros(acc_ref.shape, F32)

    def update(kb, vb, causal):
        s = jnp.dot(kb, qt, preferred_element_type=F32) * scale
        s = s.reshape(-1, n_heads, n_cols)
        keep = valid[None]
        if causal:
            key_t = lax.broadcasted_iota(jnp.int32, s.shape, 0)
            query_t = lax.broadcasted_iota(jnp.int32, s.shape, 2) % t_new
            keep = keep & (key_t <= query_t)
        s = jnp.where(keep, s, -jnp.inf)
        m_old = m_ref[...]
        m_new = jnp.maximum(m_old, jnp.max(s, axis=0))
        m_safe = jnp.where(valid, m_new, 0.0)
        alpha = jnp.where(valid, jnp.exp(m_old - m_safe), 0.0)
        p = jnp.exp(s - m_safe[None])
        l_ref[...] = alpha * l_ref[...] + jnp.sum(p, axis=0)
        m_ref[...] = m_new
        pv = lax.dot_general(p.reshape(-1, n_cols).astype(BF16), vb, (((0,), (0,)), ((), ())),
                             preferred_element_type=F32)
        acc_ref[...] = _diag_column(alpha, valid) * acc_ref[...] + pv

    update(jnp.concatenate([r[...].astype(BF16) for r in k_refs], axis=0),
           jnp.concatenate([r[...].astype(BF16) for r in v_refs], axis=0), causal=False)

    @pl.when(j == pl.num_programs(1) - 1)
    def _():
        update(kn_ref[...].astype(BF16), vn_ref[...].astype(BF16), causal=True)
        lam = _lambda(lp_ref[...], lam_init)
        out = acc_ref[...] * (1.0 / _diag_column(l_ref[...], valid))
        for h in range(n_heads):
            r0 = 2 * h * t_new
            o = out[r0:r0 + t_new, :] - lam * out[r0 + t_new:r0 + 2 * t_new, :]
            o_ref[:, h * V_DIM:(h + 1) * V_DIM] = _rms(o, g_ref[...]) * (1.0 - lam_init)


def _attn_sample(page_table, q_t, cache_k, cache_v, k_new, v_new, lam_params, g_out, layer, n_heads, t_new,
                 lam_init):
    dec_b, n_pages = page_table.shape
    page_rows = cache_k.shape[2]
    n_cols = q_t.shape[2]
    nps = _pick_tile(n_pages, 4, 1)

    def page_spec(i):
        return pl.BlockSpec((None, None, page_rows, V_DIM), lambda b, j, pt: (layer, pt[b, j * nps + i], 0, 0))

    new_spec = pl.BlockSpec((None, t_new * n_heads, V_DIM), lambda b, j, pt: (b, 0, 0))
    grid_spec = pltpu.PrefetchScalarGridSpec(
        num_scalar_prefetch=1, grid=(dec_b, n_pages // nps),
        in_specs=([pl.BlockSpec((4, HEAD_DIM), lambda b, j, pt: (0, 0)),
                   pl.BlockSpec((1, V_DIM), lambda b, j, pt: (0, 0)),
                   pl.BlockSpec((None, 2 * HEAD_DIM, n_cols), lambda b, j, pt: (b, 0, 0))]
                  + [page_spec(i) for i in range(nps)] + [page_spec(i) for i in range(nps)]
                  + [new_spec, new_spec]),
        out_specs=pl.BlockSpec((None, t_new, n_heads * V_DIM), lambda b, j, pt: (b, 0, 0)),
        scratch_shapes=[pltpu.VMEM((n_heads, n_cols), F32), pltpu.VMEM((n_heads, n_cols), F32),
                        pltpu.VMEM((n_cols, V_DIM), F32)])
    return pl.pallas_call(
        functools.partial(_attn_sample_body, n_pages_step=nps, n_heads=n_heads, t_new=t_new, lam_init=lam_init),
        grid_spec=grid_spec,
        out_shape=jax.ShapeDtypeStruct((dec_b, t_new, n_heads * V_DIM), F32),
        compiler_params=_params("parallel", "arbitrary"), name="diff_attn_sample")(
            page_table, lam_params, g_out, q_t, *([cache_k] * nps), *([cache_v] * nps), k_new, v_new)


def _conv_body(*refs, tt, has_state, row_chunk):
    if has_state:
        (sb_ref, sc_ref, sh_ref, cv_ref, cg_ref, ss_ref, cs_ref, sw_ref, cw_ref, cb_ref, lg_ref, lb_ref,
         ys_ref, yc_ref, so_ref, co_ref, ue_ref, ce_ref, acc_ref) = refs
    else:
        (sb_ref, sc_ref, sh_ref, cv_ref, cg_ref, sw_ref, cw_ref, cb_ref, lg_ref, lb_ref,
         ys_ref, yc_ref, so_ref, co_ref, ue_ref, ce_ref, acc_ref) = refs
    t = pl.program_id(1)
    width = ue_ref.shape[1]

    @pl.when(t == 0)
    def _():
        ue_ref[0:SHORT_PAD, :] = jnp.zeros((SHORT_PAD, width), F32)
        ce_ref[0:CONF_PAD, :] = jnp.zeros((CONF_PAD, width), F32)
        if has_state:
            ue_ref[SHORT_PAD - (SHORT_W - 1):SHORT_PAD, :] = ss_ref[0]
            ce_ref[CONF_PAD - (CONF_W - 1):CONF_PAD, :] = cs_ref[0]

    ue_ref[SHORT_PAD:SHORT_PAD + tt, :] = sc_ref[...] * sh_ref[...]
    ce_ref[CONF_PAD:CONF_PAD + tt, :] = cv_ref[...] * jax.nn.sigmoid(cg_ref[...])

    conv = jnp.zeros((tt, width), F32)
    for jt in range(SHORT_W):
        off = SHORT_PAD - (SHORT_W - 1) + jt
        conv = conv + sw_ref[jt:jt + 1, :] * ue_ref[off:off + tt, :]
    ys_ref[...] = (sb_ref[...] * conv).astype(ys_ref.dtype)

    base = CONF_PAD - (CONF_W - 1)
    for r0 in range(0, tt, row_chunk):
        for c0 in range(0, width, LANES):
            acc = jnp.zeros((row_chunk, LANES), F32)
            for b in range(SUBLANES_F32):
                n_a = (CONF_W - 1 - b) // SUBLANES_F32 + 1
                rows = row_chunk + SUBLANES_F32 * (n_a - 1)
                win = ce_ref[r0 + base + b:r0 + base + b + rows, c0:c0 + LANES]
                for a in range(n_a):
                    jt = SUBLANES_F32 * a + b
                    acc = acc + cw_ref[jt:jt + 1, c0:c0 + LANES] * win[SUBLANES_F32 * a:SUBLANES_F32 * a + row_chunk, :]
            acc_ref[r0:r0 + row_chunk, c0:c0 + LANES] = acc
    y = acc_ref[...] + cb_ref[...]
    mu = jnp.mean(y, axis=-1, keepdims=True)
    var = jnp.mean(jnp.square(y - mu), axis=-1, keepdims=True)
    y = (y - mu) * lax.rsqrt(var + LN_EPS) * lg_ref[...] + lb_ref[...]
    yc_ref[...] = (y * jax.nn.sigmoid(y)).astype(yc_ref.dtype)

    @pl.when(t == pl.num_programs(1) - 1)
    def _():
        so_ref[0] = ue_ref[SHORT_PAD + tt - (SHORT_W - 1):SHORT_PAD + tt, :]
        co_ref[0] = ce_ref[CONF_PAD + tt - (CONF_W - 1):CONF_PAD + tt, :]

    @pl.when(t < pl.num_programs(1) - 1)
    def _():
        ue_ref[0:SHORT_PAD, :] = ue_ref[tt:tt + SHORT_PAD, :]
        ce_ref[0:CONF_PAD, :] = ce_ref[tt:tt + CONF_PAD, :]


def _convs(z, row0, n_seq, seq, width, col_blk0, weights, states, out_dtype, name):
    sw, cw, cb, lg, lb = weights
    tt = _pick_tile(seq, 256, SUBLANES_F32)
    nt = seq // tt
    rb0 = row0 // tt
    row_chunk = _pick_tile(tt, 64, SUBLANES_F32)
    assert nt == 1 or tt >= CONF_PAD
    has_state = states is not None

    def zspec(k):
        return pl.BlockSpec((tt, width), lambda b, t: (rb0 + b * nt + t, col_blk0 + k))

    def full(a):
        return pl.BlockSpec(a.shape, lambda b, t: (0,) * a.ndim)

    in_specs = [zspec(k) for k in range(5)]
    args = [z] * 5
    if has_state:
        in_specs += [pl.BlockSpec((1, SHORT_W - 1, width), lambda b, t: (b, 0, 0)),
                     pl.BlockSpec((1, CONF_W - 1, width), lambda b, t: (b, 0, 0))]
        args += list(states)
    in_specs += [full(a) for a in (sw, cw, cb, lg, lb)]
    args += [sw, cw, cb, lg, lb]
    yspec = pl.BlockSpec((tt, width), lambda b, t: (b * nt + t, 0))
    return pl.pallas_call(
        functools.partial(_conv_body, tt=tt, has_state=has_state, row_chunk=row_chunk),
        grid=(n_seq, nt), in_specs=in_specs,
        out_specs=[yspec, yspec,
                   pl.BlockSpec((1, SHORT_W - 1, width), lambda b, t: (b, 0, 0)),
                   pl.BlockSpec((1, CONF_W - 1, width), lambda b, t: (b, 0, 0))],
        out_shape=[jax.ShapeDtypeStruct((n_seq * seq, width), out_dtype),
                   jax.ShapeDtypeStruct((n_seq * seq, width), out_dtype),
                   jax.ShapeDtypeStruct((n_seq, SHORT_W - 1, width), F32),
                   jax.ShapeDtypeStruct((n_seq, CONF_W - 1, width), F32)],
        scratch_shapes=[pltpu.VMEM((SHORT_PAD + tt, width), F32), pltpu.VMEM((CONF_PAD + tt, width), F32),
                        pltpu.VMEM((tt, width), F32)],
        compiler_params=_params("parallel", "arbitrary"), name=name)(*args)


def _rope_tables(positions):
    half = HEAD_DIM // 2
    inv_freq = jnp.power(ROPE_THETA, -jnp.arange(half, dtype=F32) * 2.0 / HEAD_DIM)
    ang = positions.astype(F32)[:, None] * inv_freq[None, :]
    cos, sin = jnp.cos(ang), jnp.sin(ang)
    return jnp.concatenate([cos, cos], axis=-1), jnp.concatenate([-sin, sin], axis=-1)


def kernel(x_prompt, x_sample, cache_k, cache_v, state_short, state_conf, page_table, norm_ffn1, ffn1_w_gate, ffn1_w_up, ffn1_w_down, norm_mix, w_in, q_norm, k_norm, lambda_q1, lambda_k1, lambda_q2, lambda_k2, attn_out_norm, short_w, conf_w, conf_b, conf_ln_g, conf_ln_b, w_out, norm_ffn2, ffn2_w_gate, ffn2_w_up, ffn2_w_down):
    batch, seq, d_model = x_prompt.shape
    dec_b, dec_s, _ = x_sample.shape
    depth, n_pool, page, n_heads, _ = cache_k.shape
    past_len = page_table.shape[1] * page
    qk_cols = n_heads * 2 * HEAD_DIM
    d_short = short_w.shape[2]
    m_prompt, m_sample = batch * seq, dec_b * dec_s
    width = n_heads * V_DIM

    bf = lambda a: a.astype(BF16)
    wg1, wu1, wd1 = bf(ffn1_w_gate), bf(ffn1_w_up), bf(ffn1_w_down)
    wg2, wu2, wd2 = bf(ffn2_w_gate), bf(ffn2_w_up), bf(ffn2_w_down)
    w_in_b, w_out_b = bf(w_in), bf(w_out)
    cache_k2 = cache_k.reshape(depth, n_pool, page * n_heads, V_DIM)
    cache_v2 = cache_v.reshape(depth, n_pool, page * n_heads, V_DIM)

    positions = jnp.concatenate([jnp.tile(jnp.arange(seq), batch), jnp.tile(past_len + jnp.arange(dec_s), dec_b)])
    cos, sin = _rope_tables(positions)
    n_maps = 2 * n_heads
    eye = jnp.eye(2, dtype=F32)

    x = jnp.concatenate([x_prompt.reshape(m_prompt, d_model), x_sample.reshape(m_sample, d_model)], axis=0)
    row = lambda a, l: a[l].reshape(1, -1)

    outs = {k: [] for k in ("kp", "vp", "sp", "cp", "ks", "vs", "ss", "cs")}
    delta = None
    for l in range(depth):
        lam_init = 0.8 - 0.6 * math.exp(-0.3 * l)
        lam_params = jnp.stack([lambda_q1[l], lambda_k1[l], lambda_q2[l], lambda_k2[l]])
        g_out = row(attn_out_norm, l)
        conv_w = (short_w[l], conf_w[l], row(conf_b, l), row(conf_ln_g, l), row(conf_ln_b, l))

        if delta is None:
            h = _norm(x, row(norm_ffn1, l))
        else:
            x, h = _add_norm(x, delta, row(norm_ffn1, l))
        delta = _ffn(h, wg1, wu1, wd1, l)
        x, h = _add_norm(x, delta, row(norm_mix, l))

        z = _matmul(h, w_in_b, l, "w_in_proj")
        qk = _qk_prep(z, cos, sin, jnp.stack([q_norm[l], k_norm[l]]), qk_cols)

        attn_p = _attn_prompt(qk, z, lam_params, g_out, batch, seq, n_heads, lam_init)

        q_s = qk[m_prompt:, :qk_cols].reshape(dec_b, dec_s, n_heads, 2, HEAD_DIM)
        q_t = jnp.einsum("bthmd,mn->bmdhnt", q_s, eye).reshape(dec_b, 2 * HEAD_DIM, n_maps * dec_s).astype(BF16)
        k_s = qk[m_prompt:, qk_cols:].reshape(dec_b, dec_s * n_heads, V_DIM)
        v_s = z[m_prompt:, 2 * qk_cols:2 * qk_cols + width].reshape(dec_b, dec_s * n_heads, V_DIM)
        attn_s = _attn_sample(page_table, q_t, cache_k2, cache_v2, k_s, v_s,
                              lam_params, g_out, l, n_heads, dec_s, lam_init)

        conv_blk0 = (2 * qk_cols + width) // d_short
        ys_p, yc_p, sp, cp = _convs(z, 0, batch, seq, d_short, conv_blk0, conv_w, None, BF16, "convs_prompt")
        ys_s, yc_s, ss, cs = _convs(z, m_prompt, dec_b, dec_s, d_short, conv_blk0, conv_w,
                                    (state_short[l], state_conf[l]), F32, "convs_sample")

        mix = jnp.concatenate([
            jnp.concatenate([attn_p, ys_p, yc_p], axis=1),
            jnp.concatenate([attn_s.reshape(m_sample, width), ys_s, yc_s], axis=1).astype(BF16)], axis=0)
        delta = _matmul(mix, w_out_b, l, "w_out_proj")
        x, h = _add_norm(x, delta, row(norm_ffn2, l))
        delta = _ffn(h, wg2, wu2, wd2, l)

        k_all = qk[:, qk_cols:]
        v_all = z[:, 2 * qk_cols:2 * qk_cols + width]
        outs["kp"].append(k_all[:m_prompt].reshape(batch, seq, n_heads, V_DIM))
        outs["vp"].append(v_all[:m_prompt].reshape(batch, seq, n_heads, V_DIM))
        outs["ks"].append(k_all[m_prompt:].reshape(dec_b, dec_s, n_heads, V_DIM))
        outs["vs"].append(v_all[m_prompt:].reshape(dec_b, dec_s, n_heads, V_DIM))
        outs["sp"].append(sp)
        outs["cp"].append(cp)
        outs["ss"].append(ss)
        outs["cs"].append(cs)

    x = _add(x, delta)
    st = {k: jnp.stack(v) for k, v in outs.items()}
    return (x[:m_prompt].reshape(batch, seq, d_model), x[m_prompt:].reshape(dec_b, dec_s, d_model),
            st["kp"], st["vp"], st["sp"], st["cp"], st["ks"], st["vs"], st["ss"], st["cs"])
```

```python
import functools
import math

import jax
import jax.numpy as jnp
from jax import lax
from jax.experimental import pallas as pl
from jax.experimental.pallas import tpu as pltpu

F32 = jnp.float32
BF16 = jnp.bfloat16

HEAD_DIM = 128
V_DIM = 2 * HEAD_DIM
SHORT_W = 3
CONF_W = 31
ROPE_THETA = 10000.0
NORM_EPS = 1e-6
LN_EPS = 1e-5

LANES = 128
SUBLANES_F32 = 8
SUBLANES_BF16 = 16
VMEM_LIMIT_BYTES = 56 * 1024 * 1024

SHORT_PAD = 8
CONF_PAD = 32


def _pick_tile(n, cap, align):
    best = None
    for d in range(align, min(n, cap) + 1, align):
        if n % d == 0:
            best = d
    if best is None:
        raise ValueError(f"no tile for n={n} cap={cap} align={align}")
    return best


def _params(*sem):
    return pltpu.CompilerParams(dimension_semantics=sem, vmem_limit_bytes=VMEM_LIMIT_BYTES)


def _rms(x, g):
    ms = jnp.mean(x * x, axis=-1, keepdims=True)
    return x * lax.rsqrt(ms + NORM_EPS) * g


def _norm_body(x_ref, g_ref, h_ref):
    h_ref[...] = _rms(x_ref[...], g_ref[...]).astype(BF16)


def _row_spec(tr, d):
    return pl.BlockSpec((tr, d), lambda i: (i, 0))


def _norm(x, g):
    m, d = x.shape
    tr = _pick_tile(m, 192, SUBLANES_BF16)
    return pl.pallas_call(
        _norm_body, grid=(m // tr,),
        in_specs=[_row_spec(tr, d), pl.BlockSpec((1, d), lambda i: (0, 0))],
        out_specs=_row_spec(tr, d),
        out_shape=jax.ShapeDtypeStruct((m, d), BF16),
        compiler_params=_params("parallel"), name="rmsnorm")(x, g)


def _ffn_up_body(h_ref, wg_ref, wu_ref, a_ref, w_ref):
    tf = wg_ref.shape[1]
    w_ref[:, :tf] = wg_ref[...].astype(BF16)
    w_ref[:, tf:] = wu_ref[...].astype(BF16)
    r = jnp.dot(h_ref[...], w_ref[...], preferred_element_type=F32)
    g, u = r[:, :tf], r[:, tf:]
    a_ref[...] = (g * jax.nn.sigmoid(g) * u * 0.5).astype(BF16)


def _ffn_up(h, wg, wu, layer):
    m, d = h.shape
    d_ff = wg.shape[2]
    tm = _pick_tile(m, 1408, SUBLANES_BF16)
    tf = _pick_tile(d_ff, 256, LANES)
    return pl.pallas_call(
        _ffn_up_body, grid=(m // tm, d_ff // tf),
        in_specs=[pl.BlockSpec((tm, d), lambda i, f: (i, 0)),
                  pl.BlockSpec((None, d, tf), lambda i, f: (layer, 0, f)),
                  pl.BlockSpec((None, d, tf), lambda i, f: (layer, 0, f))],
        out_specs=pl.BlockSpec((tm, tf), lambda i, f: (i, f)),
        out_shape=jax.ShapeDtypeStruct((m, d_ff), BF16),
        scratch_shapes=[pltpu.VMEM((d, 2 * tf), BF16)],
        compiler_params=_params("parallel", "arbitrary"), name="swiglu_up")(h, wg, wu)


def _proj_stream_body(a_ref, w_ref, x_ref, o_ref):
    o_ref[...] = x_ref[...] + jnp.dot(a_ref[...], w_ref[...], preferred_element_type=F32)


def _proj_stream(a, w, x, layer, name):
    m, k = a.shape
    n = w.shape[2]
    tm = _pick_tile(m, 704, SUBLANES_BF16)
    tn = _pick_tile(n, 256, LANES)
    return pl.pallas_call(
        _proj_stream_body, grid=(m // tm, n // tn),
        in_specs=[pl.BlockSpec((tm, k), lambda i, j: (i, 0)),
                  pl.BlockSpec((None, k, tn), lambda i, j: (layer, 0, j)),
                  pl.BlockSpec((tm, tn), lambda i, j: (i, j))],
        out_specs=pl.BlockSpec((tm, tn), lambda i, j: (i, j)),
        out_shape=jax.ShapeDtypeStruct((m, n), F32),
        compiler_params=_params("parallel", "arbitrary"), name=name)(a, w, x)


def _proj_resident_body(a_ref, w_ref, *rest, residual):
    x_ref = rest[0] if residual else None
    o_ref, wb_ref = rest[-2:]

    @pl.when(pl.program_id(1) == 0)
    def _():
        wb_ref[...] = w_ref[...].astype(BF16)

    acc = jnp.dot(a_ref[...], wb_ref[...], preferred_element_type=F32)
    o_ref[...] = acc + x_ref[...] if residual else acc


def _proj_resident(a, w, x, layer, name):
    m, k = a.shape
    n = w.shape[2]
    tm = _pick_tile(m, 704, SUBLANES_BF16)
    tn = _pick_tile(n, 512, LANES)
    residual = x is not None
    io_spec = pl.BlockSpec((tm, tn), lambda j, i: (i, j))
    return pl.pallas_call(
        functools.partial(_proj_resident_body, residual=residual), grid=(n // tn, m // tm),
        in_specs=[pl.BlockSpec((tm, k), lambda j, i: (i, 0)),
                  pl.BlockSpec((None, k, tn), lambda j, i: (layer, 0, j))] + ([io_spec] if residual else []),
        out_specs=io_spec,
        out_shape=jax.ShapeDtypeStruct((m, n), F32),
        scratch_shapes=[pltpu.VMEM((k, tn), BF16)],
        compiler_params=_params("parallel", "arbitrary"), name=name)(*([a, w] + ([x] if residual else [])))


def _qk_body(z_ref, cos_ref, sin_ref, g_ref, o_ref, *, n_q_chunks):
    cos = cos_ref[...]
    sin = sin_ref[...]
    n_chunks = o_ref.shape[1] // HEAD_DIM
    for c in range(n_chunks):
        gi = 0 if c < n_q_chunks else 1
        y = _rms(z_ref[:, c * HEAD_DIM:(c + 1) * HEAD_DIM], g_ref[gi:gi + 1, :])
        o_ref[:, c * HEAD_DIM:(c + 1) * HEAD_DIM] = y * cos + pltpu.roll(y, HEAD_DIM // 2, 1) * sin


def _qk_prep(z, cos, sin, gains, qk_cols):
    m = z.shape[0]
    tr = _pick_tile(m, 344, SUBLANES_F32)
    width = 2 * qk_cols
    return pl.pallas_call(
        functools.partial(_qk_body, n_q_chunks=qk_cols // HEAD_DIM),
        grid=(m // tr,),
        in_specs=[pl.BlockSpec((tr, width), lambda i: (i, 0)),
                  pl.BlockSpec((tr, HEAD_DIM), lambda i: (i, 0)),
                  pl.BlockSpec((tr, HEAD_DIM), lambda i: (i, 0)),
                  pl.BlockSpec((2, HEAD_DIM), lambda i: (0, 0))],
        out_specs=pl.BlockSpec((tr, width), lambda i: (i, 0)),
        out_shape=jax.ShapeDtypeStruct((m, width), F32),
        compiler_params=_params("parallel"), name="qk_norm_rope")(z, cos, sin, gains)


def _lambda(lp, lam_init):
    s1 = jnp.sum(lp[0:1, :] * lp[1:2, :], axis=-1, keepdims=True)
    s2 = jnp.sum(lp[2:3, :] * lp[3:4, :], axis=-1, keepdims=True)
    return jnp.exp(s1) - jnp.exp(s2) + lam_init


def _attn_prompt_body(lp_ref, g_ref, q_ref, k_ref, v_ref, o_ref, kb_ref, vb_ref, *, tq, lam_init):
    seq = q_ref.shape[0]
    scale = HEAD_DIM ** -0.5
    lam = _lambda(lp_ref[...], lam_init)
    kb_ref[...] = k_ref[...].astype(BF16)
    vb_ref[...] = v_ref[...].astype(BF16)
    for qi in range(seq // tq):
        n = (qi + 1) * tq
        qb = q_ref[qi * tq:(qi + 1) * tq, :].astype(BF16)
        row = lax.broadcasted_iota(jnp.int32, (tq, n), 0) + qi * tq
        col = lax.broadcasted_iota(jnp.int32, (tq, n), 1)
        visible = col <= row
        probs = []
        for mp in range(2):
            lo, hi = mp * HEAD_DIM, (mp + 1) * HEAD_DIM
            s = lax.dot_general(qb[:, lo:hi], kb_ref[0:n, lo:hi], (((1,), (1,)), ((), ())),
                                preferred_element_type=F32) * scale
            s = jnp.where(visible, s, -jnp.inf)
            p = jnp.exp(s - jnp.max(s, axis=-1, keepdims=True))
            probs.append((p, 1.0 / jnp.sum(p, axis=-1, keepdims=True)))
        (p1, r1), (p2, r2) = probs
        w = (p1 * r1 - p2 * (lam * r2)).astype(BF16)
        o = jnp.dot(w, vb_ref[0:n, :], preferred_element_type=F32)
        o = _rms(o, g_ref[...]) * (1.0 - lam_init)
        o_ref[qi * tq:(qi + 1) * tq, :] = o.astype(BF16)


def _attn_prompt(qk, z, lam_params, g_out, batch, seq, n_heads, lam_init, mix_shape):
    k_blk0 = n_heads
    v_blk0 = 2 * n_heads
    tq = _pick_tile(seq, 256, SUBLANES_BF16)
    return pl.pallas_call(
        functools.partial(_attn_prompt_body, tq=tq, lam_init=lam_init),
        grid=(batch, n_heads),
        in_specs=[pl.BlockSpec((4, HEAD_DIM), lambda b, h: (0, 0)),
                  pl.BlockSpec((1, V_DIM), lambda b, h: (0, 0)),
                  pl.BlockSpec((seq, V_DIM), lambda b, h: (b, h)),
                  pl.BlockSpec((seq, V_DIM), lambda b, h: (b, k_blk0 + h)),
                  pl.BlockSpec((seq, V_DIM), lambda b, h: (b, v_blk0 + h))],
        out_specs=pl.BlockSpec((seq, V_DIM), lambda b, h: (b, h)),
        out_shape=jax.ShapeDtypeStruct(mix_shape, BF16),
        scratch_shapes=[pltpu.VMEM((seq, V_DIM), BF16), pltpu.VMEM((seq, V_DIM), BF16)],
        compiler_params=_params("parallel", "parallel"), name="diff_attn_prompt")(
            lam_params, g_out, qk, qk, z)


def _diag_column(x, valid):
    n = x.shape[1]
    r = jnp.sum(jnp.where(valid, x, 0.0), axis=0, keepdims=True)
    eye = lax.broadcasted_iota(jnp.int32, (n, n), 0) == lax.broadcasted_iota(jnp.int32, (n, n), 1)
    return jnp.sum(jnp.where(eye, jnp.broadcast_to(r, (n, n)), 0.0), axis=1, keepdims=True)


def _attn_sample_body(pt_ref, lp_ref, g_ref, qt_ref, *rest, n_pages_step, n_heads, t_new, lam_init):
    k_refs = rest[:n_pages_step]
    v_refs = rest[n_pages_step:2 * n_pages_step]
    kn_ref, vn_ref, o_ref, m_ref, l_ref, acc_ref = rest[2 * n_pages_step:]
    j = pl.program_id(1)
    scale = HEAD_DIM ** -0.5
    qt = qt_ref[...]
    n_cols = qt.shape[1]
    col_head = lax.broadcasted_iota(jnp.int32, (n_heads, n_cols), 1) // (2 * t_new)
    valid = col_head == lax.broadcasted_iota(jnp.int32, (n_heads, n_cols), 0)

    @pl.when(j == 0)
    def _():
        m_ref[...] = jnp.full(m_ref.shape, -jnp.inf, F32)
        l_ref[...] = jnp.zeros(l_ref.shape, F32)
        acc_ref[...] = jnp.zeros(acc_ref.shape, F32)

    def update(kb, vb, causal):
        s = jnp.dot(kb, qt, preferred_element_type=F32) * scale
        s = s.reshape(-1, n_heads, n_cols)
        keep = valid[None]
        if causal:
            key_t = lax.broadcasted_iota(jnp.int32, s.shape, 0)
            query_t = lax.broadcasted_iota(jnp.int32, s.shape, 2) % t_new
            keep = keep & (key_t <= query_t)
        s = jnp.where(keep, s, -jnp.inf)
        m_old = m_ref[...]
        m_new = jnp.maximum(m_old, jnp.max(s, axis=0))
        m_safe = jnp.where(valid, m_new, 0.0)
        alpha = jnp.where(valid, jnp.exp(m_old - m_safe), 0.0)
        p = jnp.exp(s - m_safe[None])
        l_ref[...] = alpha * l_ref[...] + jnp.sum(p, axis=0)
        m_ref[...] = m_new
        pv = lax.dot_general(p.reshape(-1, n_cols).astype(BF16), vb, (((0,), (0,)), ((), ())),
                             preferred_element_type=F32)
        acc_ref[...] = _diag_column(alpha, valid) * acc_ref[...] + pv

    update(jnp.concatenate([r[...].astype(BF16) for r in k_refs], axis=0),
           jnp.concatenate([r[...].astype(BF16) for r in v_refs], axis=0), causal=False)

    @pl.when(j == pl.num_programs(1) - 1)
    def _():
        update(kn_ref[...].astype(BF16), vn_ref[...].astype(BF16), causal=True)
        lam = _lambda(lp_ref[...], lam_init)
        out = acc_ref[...] * (1.0 / _diag_column(l_ref[...], valid))
        for h in range(n_heads):
            r0 = 2 * h * t_new
            o = out[r0:r0 + t_new, :] - lam * out[r0 + t_new:r0 + 2 * t_new, :]
            o_ref[:, h * V_DIM:(h + 1) * V_DIM] = _rms(o, g_ref[...]) * (1.0 - lam_init)


def _attn_sample(page_table, q_t, cache_k, cache_v, k_new, v_new, lam_params, g_out, layer, n_heads, t_new,
                 lam_init):
    dec_b, n_pages = page_table.shape
    page_rows = cache_k.shape[2]
    n_cols = q_t.shape[2]
    nps = _pick_tile(n_pages, 4, 1)

    def page_spec(i):
        return pl.BlockSpec((None, None, page_rows, V_DIM), lambda b, j, pt: (layer, pt[b, j * nps + i], 0, 0))

    new_spec = pl.BlockSpec((None, t_new * n_heads, V_DIM), lambda b, j, pt: (b, 0, 0))
    grid_spec = pltpu.PrefetchScalarGridSpec(
        num_scalar_prefetch=1, grid=(dec_b, n_pages // nps),
        in_specs=([pl.BlockSpec((4, HEAD_DIM), lambda b, j, pt: (0, 0)),
                   pl.BlockSpec((1, V_DIM), lambda b, j, pt: (0, 0)),
                   pl.BlockSpec((None, 2 * HEAD_DIM, n_cols), lambda b, j, pt: (b, 0, 0))]
                  + [page_spec(i) for i in range(nps)] + [page_spec(i) for i in range(nps)]
                  + [new_spec, new_spec]),
        out_specs=pl.BlockSpec((None, t_new, n_heads * V_DIM), lambda b, j, pt: (b, 0, 0)),
        scratch_shapes=[pltpu.VMEM((n_heads, n_cols), F32), pltpu.VMEM((n_heads, n_cols), F32),
                        pltpu.VMEM((n_cols, V_DIM), F32)])
    return pl.pallas_call(
        functools.partial(_attn_sample_body, n_pages_step=nps, n_heads=n_heads, t_new=t_new, lam_init=lam_init),
        grid_spec=grid_spec,
        out_shape=jax.ShapeDtypeStruct((dec_b, t_new, n_heads * V_DIM), F32),
        compiler_params=_params("parallel", "arbitrary"), name="diff_attn_sample")(
            page_table, lam_params, g_out, q_t, *([cache_k] * nps), *([cache_v] * nps), k_new, v_new)


def _conv_body(*refs, tt, has_state, row_chunk):
    sb_ref, sc_ref, sh_ref, cv_ref, cg_ref = refs[:5]
    ss_ref, cs_ref = refs[5:7] if has_state else (None, None)
    sw_ref, cw_ref, cb_ref, lg_ref, lb_ref = refs[7:12] if has_state else refs[5:10]
    y_ref, so_ref, co_ref, ue_ref, ce_ref, acc_ref = refs[-6:]
    t = pl.program_id(1)
    width = ue_ref.shape[1]

    @pl.when(t == 0)
    def _():
        ue_ref[0:SHORT_PAD, :] = jnp.zeros((SHORT_PAD, width), F32)
        ce_ref[0:CONF_PAD, :] = jnp.zeros((CONF_PAD, width), F32)
        if has_state:
            ue_ref[SHORT_PAD - (SHORT_W - 1):SHORT_PAD, :] = ss_ref[0]
            ce_ref[CONF_PAD - (CONF_W - 1):CONF_PAD, :] = cs_ref[0]

    ue_ref[SHORT_PAD:SHORT_PAD + tt, :] = sc_ref[...] * sh_ref[...]
    ce_ref[CONF_PAD:CONF_PAD + tt, :] = cv_ref[...] * jax.nn.sigmoid(cg_ref[...])

    conv = jnp.zeros((tt, width), F32)
    for jt in range(SHORT_W):
        off = SHORT_PAD - (SHORT_W - 1) + jt
        conv = conv + sw_ref[jt:jt + 1, :] * ue_ref[off:off + tt, :]
    y_ref[:, :width] = (sb_ref[...] * conv).astype(y_ref.dtype)

    base = CONF_PAD - (CONF_W - 1)
    for r0 in range(0, tt, row_chunk):
        for c0 in range(0, width, LANES):
            acc = jnp.zeros((row_chunk, LANES), F32)
            for b in range(SUBLANES_F32):
                n_a = (CONF_W - 1 - b) // SUBLANES_F32 + 1
                rows = row_chunk + SUBLANES_F32 * (n_a - 1)
                win = ce_ref[r0 + base + b:r0 + base + b + rows, c0:c0 + LANES]
                for a in range(n_a):
                    jt = SUBLANES_F32 * a + b
                    acc = acc + cw_ref[jt:jt + 1, c0:c0 + LANES] * win[SUBLANES_F32 * a:SUBLANES_F32 * a + row_chunk, :]
            acc_ref[r0:r0 + row_chunk, c0:c0 + LANES] = acc
    y = acc_ref[...] + cb_ref[...]
    mu = jnp.mean(y, axis=-1, keepdims=True)
    var = jnp.mean(jnp.square(y - mu), axis=-1, keepdims=True)
    y = (y - mu) * lax.rsqrt(var + LN_EPS) * lg_ref[...] + lb_ref[...]
    y_ref[:, width:] = (y * jax.nn.sigmoid(y)).astype(y_ref.dtype)

    @pl.when(t == pl.num_programs(1) - 1)
    def _():
        so_ref[0] = ue_ref[SHORT_PAD + tt - (SHORT_W - 1):SHORT_PAD + tt, :]
        co_ref[0] = ce_ref[CONF_PAD + tt - (CONF_W - 1):CONF_PAD + tt, :]

    @pl.when(t < pl.num_programs(1) - 1)
    def _():
        ue_ref[0:SHORT_PAD, :] = ue_ref[tt:tt + SHORT_PAD, :]
        ce_ref[0:CONF_PAD, :] = ce_ref[tt:tt + CONF_PAD, :]


def _convs(z, row0, n_seq, seq, width, col_blk0, weights, states, mix, name):
    sw, cw, cb, lg, lb = weights
    tt = _pick_tile(seq, 256, SUBLANES_F32)
    nt = seq // tt
    rb0 = row0 // tt
    row_chunk = _pick_tile(tt, 64, SUBLANES_F32)
    assert nt == 1 or tt >= CONF_PAD
    has_state = states is not None

    def zspec(k):
        return pl.BlockSpec((tt, width), lambda b, t: (rb0 + b * nt + t, col_blk0 + k))

    def full(a):
        return pl.BlockSpec(a.shape, lambda b, t: (0,) * a.ndim)

    in_specs = [zspec(k) for k in range(5)]
    args = [z] * 5
    if has_state:
        in_specs += [pl.BlockSpec((1, SHORT_W - 1, width), lambda b, t: (b, 0, 0)),
                     pl.BlockSpec((1, CONF_W - 1, width), lambda b, t: (b, 0, 0))]
        args += list(states)
    in_specs += [full(a) for a in (sw, cw, cb, lg, lb)]
    args += [sw, cw, cb, lg, lb]
    if mix is None:
        y_shape = jax.ShapeDtypeStruct((n_seq * seq, 2 * width), F32)
        y_col_blk, aliases = 0, {}
    else:
        y_shape = jax.ShapeDtypeStruct(mix.shape, mix.dtype)
        y_col_blk, aliases = mix.shape[1] // (2 * width) - 1, {len(args): 0}
        in_specs.append(pl.BlockSpec(memory_space=pl.ANY))
        args.append(mix)
    return pl.pallas_call(
        functools.partial(_conv_body, tt=tt, has_state=has_state, row_chunk=row_chunk),
        grid=(n_seq, nt), in_specs=in_specs,
        out_specs=[pl.BlockSpec((tt, 2 * width), lambda b, t: (b * nt + t, y_col_blk)),
                   pl.BlockSpec((1, SHORT_W - 1, width), lambda b, t: (b, 0, 0)),
                   pl.BlockSpec((1, CONF_W - 1, width), lambda b, t: (b, 0, 0))],
        out_shape=[y_shape,
                   jax.ShapeDtypeStruct((n_seq, SHORT_W - 1, width), F32),
                   jax.ShapeDtypeStruct((n_seq, CONF_W - 1, width), F32)],
        scratch_shapes=[pltpu.VMEM((SHORT_PAD + tt, width), F32), pltpu.VMEM((CONF_PAD + tt, width), F32),
                        pltpu.VMEM((tt, width), F32)],
        input_output_aliases=aliases,
        compiler_params=_params("parallel", "arbitrary"), name=name)(*args)


def _rope_tables(positions):
    half = HEAD_DIM // 2
    inv_freq = jnp.power(ROPE_THETA, -jnp.arange(half, dtype=F32) * 2.0 / HEAD_DIM)
    ang = positions.astype(F32)[:, None] * inv_freq[None, :]
    cos, sin = jnp.cos(ang), jnp.sin(ang)
    return jnp.concatenate([cos, cos], axis=-1), jnp.concatenate([-sin, sin], axis=-1)


def kernel(x_prompt, x_sample, cache_k, cache_v, state_short, state_conf, page_table, norm_ffn1, ffn1_w_gate, ffn1_w_up, ffn1_w_down, norm_mix, w_in, q_norm, k_norm, lambda_q1, lambda_k1, lambda_q2, lambda_k2, attn_out_norm, short_w, conf_w, conf_b, conf_ln_g, conf_ln_b, w_out, norm_ffn2, ffn2_w_gate, ffn2_w_up, ffn2_w_down):
    batch, seq, d_model = x_prompt.shape
    dec_b, dec_s, _ = x_sample.shape
    depth, n_pool, page, n_heads, _ = cache_k.shape
    past_len = page_table.shape[1] * page
    qk_cols = n_heads * 2 * HEAD_DIM
    d_short = short_w.shape[2]
    m_prompt, m_sample = batch * seq, dec_b * dec_s
    width = n_heads * V_DIM

    wd1, wd2 = ffn1_w_down.astype(BF16), ffn2_w_down.astype(BF16)
    cache_k2 = cache_k.reshape(depth, n_pool, page * n_heads, V_DIM)
    cache_v2 = cache_v.reshape(depth, n_pool, page * n_heads, V_DIM)

    positions = jnp.concatenate([jnp.tile(jnp.arange(seq), batch), jnp.tile(past_len + jnp.arange(dec_s), dec_b)])
    cos, sin = _rope_tables(positions)
    n_maps = 2 * n_heads
    eye = jnp.eye(2, dtype=F32)

    x = jnp.concatenate([x_prompt.reshape(m_prompt, d_model), x_sample.reshape(m_sample, d_model)], axis=0)
    row = lambda a, l: a[l].reshape(1, -1)

    outs = {k: [] for k in ("kp", "vp", "sp", "cp", "ks", "vs", "ss", "cs")}
    for l in range(depth):
        lam_init = 0.8 - 0.6 * math.exp(-0.3 * l)
        lam_params = jnp.stack([lambda_q1[l], lambda_k1[l], lambda_q2[l], lambda_k2[l]])
        g_out = row(attn_out_norm, l)
        conv_w = (short_w[l], conf_w[l], row(conf_b, l), row(conf_ln_g, l), row(conf_ln_b, l))

        act = _ffn_up(_norm(x, row(norm_ffn1, l)), ffn1_w_gate, ffn1_w_up, l)
        x = _proj_stream(act, wd1, x, l, "swiglu_down")

        z = _proj_resident(_norm(x, row(norm_mix, l)), w_in, None, l, "w_in_proj")
        qk = _qk_prep(z, cos, sin, jnp.stack([q_norm[l], k_norm[l]]), qk_cols)

        mix = _attn_prompt(qk, z, lam_params, g_out, batch, seq, n_heads, lam_init, x.shape)

        q_s = qk[m_prompt:, :qk_cols].reshape(dec_b, dec_s, n_heads, 2, HEAD_DIM)
        q_t = jnp.einsum("bthmd,mn->bmdhnt", q_s, eye).reshape(dec_b, 2 * HEAD_DIM, n_maps * dec_s).astype(BF16)
        k_s = qk[m_prompt:, qk_cols:].reshape(dec_b, dec_s * n_heads, V_DIM)
        v_s = z[m_prompt:, 2 * qk_cols:2 * qk_cols + width].reshape(dec_b, dec_s * n_heads, V_DIM)
        attn_s = _attn_sample(page_table, q_t, cache_k2, cache_v2, k_s, v_s,
                              lam_params, g_out, l, n_heads, dec_s, lam_init)

        conv_blk0 = (2 * qk_cols + width) // d_short
        mix, sp, cp = _convs(z, 0, batch, seq, d_short, conv_blk0, conv_w, None, mix, "convs_prompt")
        y_s, ss, cs = _convs(z, m_prompt, dec_b, dec_s, d_short, conv_blk0, conv_w,
                             (state_short[l], state_conf[l]), None, "convs_sample")
        mix_s = jnp.concatenate([attn_s.reshape(m_sample, width), y_s], axis=1).astype(BF16)
        mix = lax.dynamic_update_slice(mix, mix_s, (m_prompt, 0))

        x = _proj_resident(mix, w_out, x, l, "w_out_proj")
        act = _ffn_up(_norm(x, row(norm_ffn2, l)), ffn2_w_gate, ffn2_w_up, l)
        x = _proj_stream(act, wd2, x, l, "swiglu_down")

        k_all = qk[:, qk_cols:]
        v_all = z[:, 2 * qk_cols:2 * qk_cols + width]
        outs["kp"].append(k_all[:m_prompt].reshape(batch, seq, n_heads, V_DIM))
        outs["vp"].append(v_all[:m_prompt].reshape(batch, seq, n_heads, V_DIM))
        outs["ks"].append(k_all[m_prompt:].reshape(dec_b, dec_s, n_heads, V_DIM))
        outs["vs"].append(v_all[m_prompt:].reshape(dec_b, dec_s, n_heads, V_DIM))
        outs["sp"].append(sp)
        outs["cp"].append(cp)
        outs["ss"].append(ss)
        outs["cs"].append(cs)

    st ={k: jnp.stack(v) for k, v in outs.items()}
    return (x[:m_prompt].reshape(batch, seq, d_model), x[m_prompt:].reshape(dec_b, dec_s, d_model),
            st["kp"], st["vp"], st["sp"], st["cp"], st["ks"], st["vs"], st["ss"], st["cs"])
```

```python
import functools
import math

import jax
import jax.numpy as jnp
from jax import lax
from jax.experimental import pallas as pl
from jax.experimental.pallas import tpu as pltpu

F32 = jnp.float32
BF16 = jnp.bfloat16

HEAD_DIM = 128
V_DIM = 2 * HEAD_DIM
SHORT_W = 3
CONF_W = 31
ROPE_THETA = 10000.0
NORM_EPS = 1e-6
LN_EPS = 1e-5

LANES = 128
SUBLANES_F32 = 8
SUBLANES_BF16 = 16
VMEM_LIMIT_BYTES = 56 * 1024 * 1024

SHORT_PAD = 8
CONF_PAD = 32
PAGES_PER_STEP = 8


def _pick_tile(n, cap, align):
    best = None
    for d in range(align, min(n, cap) + 1, align):
        if n % d == 0:
            best = d
    if best is None:
        raise ValueError(f"no tile for n={n} cap={cap} align={align}")
    return best


def _params(*sem):
    return pltpu.CompilerParams(dimension_semantics=sem, vmem_limit_bytes=VMEM_LIMIT_BYTES)


def _rms(x, g):
    ms = jnp.mean(x * x, axis=-1, keepdims=True)
    return x * lax.rsqrt(ms + NORM_EPS) * g


def _norm_body(x_ref, g_ref, h_ref):
    h_ref[...] = _rms(x_ref[...], g_ref[...]).astype(BF16)


def _row_spec(tr, d):
    return pl.BlockSpec((tr, d), lambda i: (i, 0))


def _norm(x, g):
    m, d = x.shape
    tr = _pick_tile(m, 192, SUBLANES_BF16)
    return pl.pallas_call(
        _norm_body, grid=(m // tr,),
        in_specs=[_row_spec(tr, d), pl.BlockSpec((1, d), lambda i: (0, 0))],
        out_specs=_row_spec(tr, d),
        out_shape=jax.ShapeDtypeStruct((m, d), BF16),
        compiler_params=_params("parallel"), name="rmsnorm")(x, g)


def _ffn_up_body(h_ref, wg_ref, wu_ref, wd_ref, a_ref, wdb_ref, w_ref, *, n_cast_tiles):
    tf = wg_ref.shape[1]
    w_ref[:, :tf] = wg_ref[...].astype(BF16)
    w_ref[:, tf:] = wu_ref[...].astype(BF16)
    r = jnp.dot(h_ref[...], w_ref[...], preferred_element_type=F32)
    g, u = r[:, :tf], r[:, tf:]
    a_ref[...] = (g * jax.nn.sigmoid(g) * u * 0.5).astype(BF16)

    @pl.when(pl.program_id(0) < n_cast_tiles)
    def _():
        wdb_ref[...] = wd_ref[...].astype(BF16)


def _ffn_up(h, wg, wu, wd, layer):
    m, d = h.shape
    d_ff = wg.shape[2]
    tm = _pick_tile(m, 1408, SUBLANES_BF16)
    tf = _pick_tile(d_ff, 256, LANES)
    nf = d_ff // tf
    n_cast = 2 if (m // tm >= 2 and d % (2 * LANES) == 0) else 1
    tc = d // n_cast

    def wd_blk(i, f):
        live = i < n_cast
        return jnp.where(live, f, nf - 1), jnp.where(live, i, n_cast - 1)

    return pl.pallas_call(
        functools.partial(_ffn_up_body, n_cast_tiles=n_cast), grid=(m // tm, nf),
        in_specs=[pl.BlockSpec((tm, d), lambda i, f: (i, 0)),
                  pl.BlockSpec((None, d, tf), lambda i, f: (layer, 0, f)),
                  pl.BlockSpec((None, d, tf), lambda i, f: (layer, 0, f)),
                  pl.BlockSpec((None, tf, tc), lambda i, f: (layer,) + wd_blk(i, f))],
        out_specs=[pl.BlockSpec((tm, tf), lambda i, f: (i, f)),
                   pl.BlockSpec((tf, tc), wd_blk)],
        out_shape=[jax.ShapeDtypeStruct((m, d_ff), BF16), jax.ShapeDtypeStruct((d_ff, d), BF16)],
        scratch_shapes=[pltpu.VMEM((d, 2 * tf), BF16)],
        compiler_params=_params("arbitrary", "arbitrary"), name="swiglu_up")(h, wg, wu, wd)


def _proj_stream_body(a_ref, w_ref, x_ref, o_ref):
    o_ref[...] = x_ref[...] + jnp.dot(a_ref[...], w_ref[...], preferred_element_type=F32)


def _proj_stream(a, w, x, name):
    m, k = a.shape
    n = w.shape[1]
    tm = _pick_tile(m, 704, SUBLANES_BF16)
    tn = _pick_tile(n, 256, LANES)
    io_spec = pl.BlockSpec((tm, tn), lambda i, j: (i, j))
    return pl.pallas_call(
        _proj_stream_body, grid=(m // tm, n // tn),
        in_specs=[pl.BlockSpec((tm, k), lambda i, j: (i, 0)), pl.BlockSpec((k, tn), lambda i, j: (0, j)), io_spec],
        out_specs=io_spec,
        out_shape=jax.ShapeDtypeStruct((m, n), F32),
        compiler_params=_params("parallel", "arbitrary"), name=name)(a, w, x)


def _proj_cast_body(a_ref, w_ref, *rest, residual):
    x_ref = rest[0] if residual else None
    o_ref, wb_ref = rest[-2:]
    wb_ref[...] = w_ref[...].astype(BF16)
    acc = jnp.dot(a_ref[...], wb_ref[...], preferred_element_type=F32)
    o_ref[...] = acc + x_ref[...] if residual else acc


def _proj_cast(a, w, x, layer, name):
    m, k = a.shape
    n = w.shape[2]
    residual = x is not None
    tm = _pick_tile(m, 1408, SUBLANES_BF16)
    tn = _pick_tile(n, 256 if residual else 512, LANES)
    io_spec = pl.BlockSpec((tm, tn), lambda i, j: (i, j))
    return pl.pallas_call(
        functools.partial(_proj_cast_body, residual=residual), grid=(m // tm, n // tn),
        in_specs=[pl.BlockSpec((tm, k), lambda i, j: (i, 0)),
                  pl.BlockSpec((None, k, tn), lambda i, j: (layer, 0, j))] + ([io_spec] if residual else []),
        out_specs=io_spec,
        out_shape=jax.ShapeDtypeStruct((m, n), F32),
        scratch_shapes=[pltpu.VMEM((k, tn), BF16)],
        compiler_params=_params("parallel", "arbitrary"), name=name)(*([a, w] + ([x] if residual else [])))


def _qk_body(z_ref, cos_ref, sin_ref, g_ref, o_ref, *, n_q_chunks):
    cos = cos_ref[...]
    sin = sin_ref[...]
    n_chunks = o_ref.shape[1] // HEAD_DIM
    for c in range(n_chunks):
        gi = 0 if c < n_q_chunks else 1
        y = _rms(z_ref[:, c * HEAD_DIM:(c + 1) * HEAD_DIM], g_ref[gi:gi + 1, :])
        o_ref[:, c * HEAD_DIM:(c + 1) * HEAD_DIM] = y * cos + pltpu.roll(y, HEAD_DIM // 2, 1) * sin


def _qk_prep(z, cos, sin, gains, qk_cols):
    m = z.shape[0]
    tr = _pick_tile(m, 344, SUBLANES_F32)
    width = 2 * qk_cols
    return pl.pallas_call(
        functools.partial(_qk_body, n_q_chunks=qk_cols // HEAD_DIM),
        grid=(m // tr,),
        in_specs=[pl.BlockSpec((tr, width), lambda i: (i, 0)),
                  pl.BlockSpec((tr, HEAD_DIM), lambda i: (i, 0)),
                  pl.BlockSpec((tr, HEAD_DIM), lambda i: (i, 0)),
                  pl.BlockSpec((2, HEAD_DIM), lambda i: (0, 0))],
        out_specs=pl.BlockSpec((tr, width), lambda i: (i, 0)),
        out_shape=jax.ShapeDtypeStruct((m, width), F32),
        compiler_params=_params("parallel"), name="qk_norm_rope")(z, cos, sin, gains)


def _lambda(lp, lam_init):
    s1 = jnp.sum(lp[0:1, :] * lp[1:2, :], axis=-1, keepdims=True)
    s2 = jnp.sum(lp[2:3, :] * lp[3:4, :], axis=-1, keepdims=True)
    return jnp.exp(s1) - jnp.exp(s2) + lam_init


def _attn_prompt_body(lp_ref, g_ref, q_ref, k_ref, v_ref, o_ref, kb_ref, vb_ref, *, tq, lam_init):
    seq = q_ref.shape[0]
    scale = HEAD_DIM ** -0.5
    lam = _lambda(lp_ref[...], lam_init)
    kb_ref[...] = k_ref[...].astype(BF16)
    vb_ref[...] = v_ref[...].astype(BF16)
    tri = lax.broadcasted_iota(jnp.int32, (tq, tq), 1) <= lax.broadcasted_iota(jnp.int32, (tq, tq), 0)
    nt = (((1,), (1,)), ((), ()))
    for qi in range(seq // tq):
        n0 = qi * tq
        qb = (q_ref[n0:n0 + tq, :] * scale).astype(BF16)
        outs = []
        for mp in range(2):
            lo, hi = mp * HEAD_DIM, (mp + 1) * HEAD_DIM
            s_d = lax.dot_general(qb[:, lo:hi], kb_ref[n0:n0 + tq, lo:hi], nt, preferred_element_type=F32)
            s_d = jnp.where(tri, s_d, -jnp.inf)
            mx = jnp.max(s_d, axis=-1, keepdims=True)
            if qi > 0:
                s_o = lax.dot_general(qb[:, lo:hi], kb_ref[0:n0, lo:hi], nt, preferred_element_type=F32)
                mx = jnp.maximum(mx, jnp.max(s_o, axis=-1, keepdims=True))
            p_d = jnp.exp(s_d - mx)
            l = jnp.sum(p_d, axis=-1, keepdims=True)
            pv = jnp.dot(p_d.astype(BF16), vb_ref[n0:n0 + tq, :], preferred_element_type=F32)
            if qi > 0:
                p_o = jnp.exp(s_o - mx)
                l = l + jnp.sum(p_o, axis=-1, keepdims=True)
                pv = pv + jnp.dot(p_o.astype(BF16), vb_ref[0:n0, :], preferred_element_type=F32)
            outs.append(pv * (1.0 / l))
        o = outs[0] - lam * outs[1]
        o = _rms(o, g_ref[...]) * (1.0 - lam_init)
        o_ref[n0:n0 + tq, :] = o.astype(BF16)


def _attn_prompt(qk, z, lam_params, g_out, batch, seq, n_heads, lam_init, mix_shape):
    k_blk0 = n_heads
    v_blk0 = 2 * n_heads
    tq = _pick_tile(seq, 256, SUBLANES_BF16)
    return pl.pallas_call(
        functools.partial(_attn_prompt_body, tq=tq, lam_init=lam_init),
        grid=(batch, n_heads),
        in_specs=[pl.BlockSpec((4, HEAD_DIM), lambda b, h: (0, 0)),
                  pl.BlockSpec((1, V_DIM), lambda b, h: (0, 0)),
                  pl.BlockSpec((seq, V_DIM), lambda b, h: (b, h)),
                  pl.BlockSpec((seq, V_DIM), lambda b, h: (b, k_blk0 + h)),
                  pl.BlockSpec((seq, V_DIM), lambda b, h: (b, v_blk0 + h))],
        out_specs=pl.BlockSpec((seq, V_DIM), lambda b, h: (b, h)),
        out_shape=jax.ShapeDtypeStruct(mix_shape, BF16),
        scratch_shapes=[pltpu.VMEM((seq, V_DIM), BF16), pltpu.VMEM((seq, V_DIM), BF16)],
        compiler_params=_params("parallel", "parallel"), name="diff_attn_prompt")(
            lam_params, g_out, qk, qk, z)


def _diag_column(x, valid):
    n = x.shape[1]
    r = jnp.sum(jnp.where(valid, x, 0.0), axis=0, keepdims=True)
    eye = lax.broadcasted_iota(jnp.int32, (n, n), 0) == lax.broadcasted_iota(jnp.int32, (n, n), 1)
    return jnp.sum(jnp.where(eye, jnp.broadcast_to(r, (n, n)), 0.0), axis=1, keepdims=True)


def _attn_sample_body(pt_ref, lp_ref, g_ref, qt_ref, *rest, n_pages_step, n_heads, t_new, lam_init):
    k_refs = rest[:n_pages_step]
    v_refs = rest[n_pages_step:2 * n_pages_step]
    kn_ref, vn_ref, o_ref, m_ref, l_ref, acc_ref = rest[2 * n_pages_step:]
    j = pl.program_id(1)
    qt = qt_ref[...]
    n_cols = qt.shape[1]
    col_head = lax.broadcasted_iota(jnp.int32, (n_heads, n_cols), 1) // (2 * t_new)
    valid = col_head == lax.broadcasted_iota(jnp.int32, (n_heads, n_cols), 0)

    @pl.when(j == 0)
    def _():
        m_ref[...] = jnp.full(m_ref.shape, -jnp.inf, F32)
        l_ref[...] = jnp.zeros(l_ref.shape, F32)
        acc_ref[...] = jnp.zeros(acc_ref.shape, F32)

    def update(kb, vb, causal):
        s = jnp.dot(kb, qt, preferred_element_type=F32)
        s = s.reshape(-1, n_heads, n_cols)
        keep = valid[None]
        if causal:
            key_t = lax.broadcasted_iota(jnp.int32, s.shape, 0)
            query_t = lax.broadcasted_iota(jnp.int32, s.shape, 2) % t_new
            keep = keep & (key_t <= query_t)
        s = jnp.where(keep, s, -jnp.inf)
        m_old = m_ref[...]
        m_new = jnp.maximum(m_old, jnp.max(s, axis=0))
        m_safe = jnp.where(valid, m_new, 0.0)
        alpha = jnp.where(valid, jnp.exp(m_old - m_safe), 0.0)
        p = jnp.exp(s - m_safe[None])
        l_ref[...] = alpha * l_ref[...] + jnp.sum(p, axis=0)
        m_ref[...] = m_new
        pv = lax.dot_general(p.reshape(-1, n_cols).astype(BF16), vb, (((0,), (0,)), ((), ())),
                             preferred_element_type=F32)
        acc_ref[...] = _diag_column(alpha, valid) * acc_ref[...] + pv

    update(jnp.concatenate([r[...].astype(BF16) for r in k_refs], axis=0),
           jnp.concatenate([r[...].astype(BF16) for r in v_refs], axis=0), causal=False)

    @pl.when(j == pl.num_programs(1) - 1)
    def _():
        update(kn_ref[...].astype(BF16), vn_ref[...].astype(BF16), causal=True)
        lam = _lambda(lp_ref[...], lam_init)
        out = acc_ref[...] * (1.0 / _diag_column(l_ref[...], valid))
        for h in range(n_heads):
            r0 = 2 * h * t_new
            o = out[r0:r0 + t_new, :] - lam * out[r0 + t_new:r0 + 2 * t_new, :]
            o_ref[:, h * V_DIM:(h + 1) * V_DIM] = _rms(o, g_ref[...]) * (1.0 - lam_init)


def _attn_sample(page_table, q_t, cache_k, cache_v, k_new, v_new, lam_params, g_out, layer, n_heads, t_new,
                 lam_init):
    dec_b, n_pages = page_table.shape
    page_rows = cache_k.shape[2]
    n_cols = q_t.shape[2]
    nps = _pick_tile(n_pages, PAGES_PER_STEP, 1)

    def page_spec(i):
        return pl.BlockSpec((None, None, page_rows, V_DIM), lambda b, j, pt: (layer, pt[b, j * nps + i], 0, 0))

    new_spec = pl.BlockSpec((None, t_new * n_heads, V_DIM), lambda b, j, pt: (b, 0, 0))
    grid_spec = pltpu.PrefetchScalarGridSpec(
        num_scalar_prefetch=1, grid=(dec_b, n_pages // nps),
        in_specs=([pl.BlockSpec((4, HEAD_DIM), lambda b, j, pt: (0, 0)),
                   pl.BlockSpec((1, V_DIM), lambda b, j, pt: (0, 0)),
                   pl.BlockSpec((None, 2 * HEAD_DIM, n_cols), lambda b, j, pt: (b, 0, 0))]
                  + [page_spec(i) for i in range(nps)] + [page_spec(i) for i in range(nps)]
                  + [new_spec, new_spec]),
        out_specs=pl.BlockSpec((None, t_new, n_heads * V_DIM), lambda b, j, pt: (b, 0, 0)),
        scratch_shapes=[pltpu.VMEM((n_heads, n_cols), F32), pltpu.VMEM((n_heads, n_cols), F32),
                        pltpu.VMEM((n_cols, V_DIM), F32)])
    return pl.pallas_call(
        functools.partial(_attn_sample_body, n_pages_step=nps, n_heads=n_heads, t_new=t_new, lam_init=lam_init),
        grid_spec=grid_spec,
        out_shape=jax.ShapeDtypeStruct((dec_b, t_new, n_heads * V_DIM), F32),
        compiler_params=_params("parallel", "arbitrary"), name="diff_attn_sample")(
            page_table, lam_params, g_out, q_t, *([cache_k] * nps), *([cache_v] * nps), k_new, v_new)


def _conv_body(*refs, tt, has_state, row_chunk):
    sb_ref, sc_ref, sh_ref, cv_ref, cg_ref = refs[:5]
    ss_ref, cs_ref = refs[5:7] if has_state else (None, None)
    sw_ref, cw_ref, cb_ref, lg_ref, lb_ref = refs[7:12] if has_state else refs[5:10]
    y_ref, so_ref, co_ref, ue_ref, ce_ref, acc_ref = refs[-6:]
    t = pl.program_id(1)
    width = ue_ref.shape[1]

    @pl.when(t == 0)
    def _():
        ue_ref[0:SHORT_PAD, :] = jnp.zeros((SHORT_PAD, width), F32)
        ce_ref[0:CONF_PAD, :] = jnp.zeros((CONF_PAD, width), F32)
        if has_state:
            ue_ref[SHORT_PAD - (SHORT_W - 1):SHORT_PAD, :] = ss_ref[0]
            ce_ref[CONF_PAD - (CONF_W - 1):CONF_PAD, :] = cs_ref[0]

    ue_ref[SHORT_PAD:SHORT_PAD + tt, :] = sc_ref[...] * sh_ref[...]
    ce_ref[CONF_PAD:CONF_PAD + tt, :] = cv_ref[...] * jax.nn.sigmoid(cg_ref[...])

    conv = jnp.zeros((tt, width), F32)
    for jt in range(SHORT_W):
        off = SHORT_PAD - (SHORT_W - 1) + jt
        conv = conv + sw_ref[jt:jt + 1, :] * ue_ref[off:off + tt, :]
    y_ref[:, :width] = (sb_ref[...] * conv).astype(y_ref.dtype)

    base = CONF_PAD - (CONF_W - 1)
    for r0 in range(0, tt, row_chunk):
        for c0 in range(0, width, LANES):
            acc = jnp.zeros((row_chunk, LANES), F32)
            for b in range(SUBLANES_F32):
                n_a = (CONF_W - 1 - b) // SUBLANES_F32 + 1
                rows = row_chunk + SUBLANES_F32 * (n_a - 1)
                win = ce_ref[r0 + base + b:r0 + base + b + rows, c0:c0 + LANES]
                for a in range(n_a):
                    jt = SUBLANES_F32 * a + b
                    acc = acc + cw_ref[jt:jt + 1, c0:c0 + LANES] * win[SUBLANES_F32 * a:SUBLANES_F32 * a + row_chunk, :]
            acc_ref[r0:r0 + row_chunk, c0:c0 + LANES] = acc
    y = acc_ref[...] + cb_ref[...]
    mu = jnp.mean(y, axis=-1, keepdims=True)
    var = jnp.mean(jnp.square(y - mu), axis=-1, keepdims=True)
    y = (y - mu) * lax.rsqrt(var + LN_EPS) * lg_ref[...] + lb_ref[...]
    y_ref[:, width:] = (y * jax.nn.sigmoid(y)).astype(y_ref.dtype)

    @pl.when(t == pl.num_programs(1) - 1)
    def _():
        so_ref[0] = ue_ref[SHORT_PAD + tt - (SHORT_W - 1):SHORT_PAD + tt, :]
        co_ref[0] = ce_ref[CONF_PAD + tt - (CONF_W - 1):CONF_PAD + tt, :]

    @pl.when(t < pl.num_programs(1) - 1)
    def _():
        ue_ref[0:SHORT_PAD, :] = ue_ref[tt:tt + SHORT_PAD, :]
        ce_ref[0:CONF_PAD, :] = ce_ref[tt:tt + CONF_PAD, :]


def _convs(z, row0, n_seq, seq, width, col_blk0, weights, states, mix, name):
    sw, cw, cb, lg, lb = weights
    tt = _pick_tile(seq, 256, SUBLANES_F32)
    nt = seq // tt
    rb0 = row0 // tt
    row_chunk = _pick_tile(tt, 64, SUBLANES_F32)
    assert nt == 1 or tt >= CONF_PAD
    has_state = states is not None

    def zspec(k):
        return pl.BlockSpec((tt, width), lambda b, t: (rb0 + b * nt + t, col_blk0 + k))

    def full(a):
        return pl.BlockSpec(a.shape, lambda b, t: (0,) * a.ndim)

    in_specs = [zspec(k) for k in range(5)]
    args = [z] * 5
    if has_state:
        in_specs += [pl.BlockSpec((1, SHORT_W - 1, width), lambda b, t: (b, 0, 0)),
                     pl.BlockSpec((1, CONF_W - 1, width), lambda b, t: (b, 0, 0))]
        args += list(states)
    in_specs += [full(a) for a in (sw, cw, cb, lg, lb)]
    args += [sw, cw, cb, lg, lb]
    if mix is None:
        y_shape = jax.ShapeDtypeStruct((n_seq * seq, 2 * width), F32)
        y_col_blk, aliases = 0, {}
    else:
        y_shape = jax.ShapeDtypeStruct(mix.shape, mix.dtype)
        y_col_blk, aliases = mix.shape[1] // (2 * width) - 1, {len(args): 0}
        in_specs.append(pl.BlockSpec(memory_space=pl.ANY))
        args.append(mix)
    return pl.pallas_call(
        functools.partial(_conv_body, tt=tt, has_state=has_state, row_chunk=row_chunk),
        grid=(n_seq, nt), in_specs=in_specs,
        out_specs=[pl.BlockSpec((tt, 2 * width), lambda b, t: (b * nt + t, y_col_blk)),
                   pl.BlockSpec((1, SHORT_W - 1, width), lambda b, t: (b, 0, 0)),
                   pl.BlockSpec((1, CONF_W - 1, width), lambda b, t: (b, 0, 0))],
        out_shape=[y_shape,
                   jax.ShapeDtypeStruct((n_seq, SHORT_W - 1, width), F32),
                   jax.ShapeDtypeStruct((n_seq, CONF_W - 1, width), F32)],
        scratch_shapes=[pltpu.VMEM((SHORT_PAD + tt, width), F32), pltpu.VMEM((CONF_PAD + tt, width), F32),
                        pltpu.VMEM((tt, width), F32)],
        input_output_aliases=aliases,
        compiler_params=_params("parallel", "arbitrary"), name=name)(*args)


def _rope_tables(positions):
    half = HEAD_DIM // 2
    inv_freq = jnp.power(ROPE_THETA, -jnp.arange(half, dtype=F32) * 2.0 / HEAD_DIM)
    ang = positions.astype(F32)[:, None] * inv_freq[None, :]
    cos, sin = jnp.cos(ang), jnp.sin(ang)
    return jnp.concatenate([cos, cos], axis=-1), jnp.concatenate([-sin, sin], axis=-1)


def kernel(x_prompt, x_sample, cache_k, cache_v, state_short, state_conf, page_table, norm_ffn1, ffn1_w_gate, ffn1_w_up, ffn1_w_down, norm_mix, w_in, q_norm, k_norm, lambda_q1, lambda_k1, lambda_q2, lambda_k2, attn_out_norm, short_w, conf_w, conf_b, conf_ln_g, conf_ln_b, w_out, norm_ffn2, ffn2_w_gate, ffn2_w_up, ffn2_w_down):
    batch, seq, d_model = x_prompt.shape
    dec_b, dec_s, _ = x_sample.shape
    depth, n_pool, page, n_heads, _ = cache_k.shape
    past_len = page_table.shape[1] * page
    qk_cols = n_heads * 2 * HEAD_DIM
    d_short = short_w.shape[2]
    m_prompt, m_sample = batch * seq, dec_b * dec_s
    width = n_heads * V_DIM

    cache_k2 = cache_k.reshape(depth, n_pool, page * n_heads, V_DIM)
    cache_v2 = cache_v.reshape(depth, n_pool, page * n_heads, V_DIM)

    positions = jnp.concatenate([jnp.tile(jnp.arange(seq), batch), jnp.tile(past_len + jnp.arange(dec_s), dec_b)])
    cos, sin = _rope_tables(positions)
    n_maps = 2 * n_heads
    eye = jnp.eye(2, dtype=F32)

    x = jnp.concatenate([x_prompt.reshape(m_prompt, d_model), x_sample.reshape(m_sample, d_model)], axis=0)
    row = lambda a, l: a[l].reshape(1, -1)

    outs = {k: [] for k in ("kp", "vp", "sp", "cp", "ks", "vs", "ss", "cs")}
    for l in range(depth):
        lam_init = 0.8 - 0.6 * math.exp(-0.3 * l)
        lam_params = jnp.stack([lambda_q1[l], lambda_k1[l], lambda_q2[l], lambda_k2[l]])
        g_out = row(attn_out_norm, l)
        conv_w = (short_w[l], conf_w[l], row(conf_b, l), row(conf_ln_g, l), row(conf_ln_b, l))

        act, wd_b = _ffn_up(_norm(x, row(norm_ffn1, l)), ffn1_w_gate, ffn1_w_up, ffn1_w_down, l)
        x = _proj_stream(act, wd_b, x, "swiglu_down")

        z = _proj_cast(_norm(x, row(norm_mix, l)), w_in, None, l, "w_in_proj")
        qk = _qk_prep(z, cos, sin, jnp.stack([q_norm[l], k_norm[l]]), qk_cols)

        mix = _attn_prompt(qk, z, lam_params, g_out, batch, seq, n_heads, lam_init, x.shape)

        q_s = qk[m_prompt:, :qk_cols].reshape(dec_b, dec_s, n_heads, 2, HEAD_DIM) * (HEAD_DIM ** -0.5)
        q_t = jnp.einsum("bthmd,mn->bmdhnt", q_s, eye).reshape(dec_b, 2 * HEAD_DIM, n_maps * dec_s).astype(BF16)
        k_s = qk[m_prompt:, qk_cols:].reshape(dec_b, dec_s * n_heads, V_DIM)
        v_s = z[m_prompt:, 2 * qk_cols:2 * qk_cols + width].reshape(dec_b, dec_s * n_heads, V_DIM)
        attn_s = _attn_sample(page_table, q_t, cache_k2, cache_v2, k_s, v_s,
                              lam_params, g_out, l, n_heads, dec_s, lam_init)

        conv_blk0 = (2 * qk_cols + width) // d_short
        mix, sp, cp = _convs(z, 0, batch, seq, d_short, conv_blk0, conv_w, None, mix, "convs_prompt")
        y_s, ss, cs = _convs(z, m_prompt, dec_b, dec_s, d_short, conv_blk0, conv_w,
                             (state_short[l], state_conf[l]), None, "convs_sample")
        mix_s = jnp.concatenate([attn_s.reshape(m_sample, width), y_s], axis=1).astype(BF16)
        mix = lax.dynamic_update_slice(mix, mix_s, (m_prompt, 0))

        x = _proj_cast(mix, w_out, x, l, "w_out_proj")
        act, wd_b = _ffn_up(_norm(x, row(norm_ffn2, l)), ffn2_w_gate, ffn2_w_up, ffn2_w_down, l)
        x = _proj_stream(act, wd_b, x, "swiglu_down")

        k_all = qk[:, qk_cols:]
        v_all = z[:, 2 * qk_cols:2 * qk_cols + width]
        outs["kp"].append(k_all[:m_prompt].reshape(batch, seq, n_heads, V_DIM))
        outs["vp"].append(v_all[:m_prompt].reshape(batch, seq, n_heads, V_DIM))
        outs["ks"].append(k_all[m_prompt:].reshape(dec_b, dec_s, n_heads, V_DIM))
        outs["vs"].append(v_all[m_prompt:].reshape(dec_b, dec_s, n_heads, V_DIM))
        outs["sp"].append(sp)
        outs["cp"].append(cp)
        outs["ss"].append(ss)
        outs["cs"].append(cs)

    st = {k: jnp.stack(v) for k, v in outs.items()}
    return (x[:m_prompt].reshape(batch, seq, d_model), x[m_prompt:].reshape(dec_b, dec_s, d_model),
            st["kp"], st["vp"], st["sp"], st["cp"], st["ks"], st["vs"], st["ss"], st["cs"])
```

```python
import functools
import math

import jax
import jax.numpy as jnp
from jax import lax
from jax.experimental import pallas as pl
from jax.experimental.pallas import tpu as pltpu

F32 = jnp.float32
BF16 = jnp.bfloat16

HEAD_DIM = 128
V_DIM = 2 * HEAD_DIM
SHORT_W = 3
CONF_W = 31
ROPE_THETA = 10000.0
NORM_EPS = 1e-6
LN_EPS = 1e-5

LANES = 128
SUBLANES_F32 = 8
SUBLANES_BF16 = 16
VMEM_LIMIT_BYTES = 56 * 1024 * 1024

SHORT_PAD = 8
CONF_PAD = 32
PAGES_PER_STEP = 8


def _pick_tile(n, cap, align):
    best = None
    for d in range(align, min(n, cap) + 1, align):
        if n % d == 0:
            best = d
    if best is None:
        raise ValueError(f"no tile for n={n} cap={cap} align={align}")
    return best


def _params(*sem):
    return pltpu.CompilerParams(dimension_semantics=sem, vmem_limit_bytes=VMEM_LIMIT_BYTES)


def _rms(x, g):
    ms = jnp.mean(x * x, axis=-1, keepdims=True)
    return x * lax.rsqrt(ms + NORM_EPS) * g


def _norm_body(x_ref, g_ref, h_ref):
    h_ref[...] = _rms(x_ref[...], g_ref[...]).astype(BF16)


def _row_spec(tr, d):
    return pl.BlockSpec((tr, d), lambda i: (i, 0))


def _norm(x, g):
    m, d = x.shape
    tr = _pick_tile(m, 704, SUBLANES_BF16)
    return pl.pallas_call(
        _norm_body, grid=(m // tr,),
        in_specs=[_row_spec(tr, d), pl.BlockSpec((1, d), lambda i: (0, 0))],
        out_specs=_row_spec(tr, d),
        out_shape=jax.ShapeDtypeStruct((m, d), BF16),
        compiler_params=_params("parallel"), name="rmsnorm")(x, g)


def _ffn_up_body(h_ref, wg_ref, wu_ref, wd_ref, a_ref, wdb_ref, w_ref):
    tf = wg_ref.shape[1]
    w_ref[:, :tf] = wg_ref[...].astype(BF16)
    w_ref[:, tf:] = wu_ref[...].astype(BF16)
    r = jnp.dot(h_ref[...], w_ref[...], preferred_element_type=F32)
    g, u = r[:, :tf], r[:, tf:]
    a_ref[...] = (g * jax.nn.sigmoid(g) * u * 0.5).astype(BF16)
    wdb_ref[...] = wd_ref[...].astype(BF16)


def _ffn_up(h, wg, wu, wd, layer):
    m, d = h.shape
    d_ff = wg.shape[2]
    tm = _pick_tile(m, 1408, SUBLANES_BF16)
    tf = _pick_tile(d_ff, 256, LANES)
    nf = d_ff // tf
    n_cast = 2 if (m // tm >= 2 and d % (2 * LANES) == 0) else 1
    tc = d // n_cast

    def wd_blk(i, f):
        live = i < n_cast
        return jnp.where(live, f, nf - 1), jnp.where(live, i, n_cast - 1)

    return pl.pallas_call(
        _ffn_up_body, grid=(m // tm, nf),
        in_specs=[pl.BlockSpec((tm, d), lambda i, f: (i, 0)),
                  pl.BlockSpec((None, d, tf), lambda i, f: (layer, 0, f)),
                  pl.BlockSpec((None, d, tf), lambda i, f: (layer, 0, f)),
                  pl.BlockSpec((None, tf, tc), lambda i, f: (layer,) + wd_blk(i, f))],
        out_specs=[pl.BlockSpec((tm, tf), lambda i, f: (i, f)),
                   pl.BlockSpec((tf, tc), wd_blk)],
        out_shape=[jax.ShapeDtypeStruct((m, d_ff), BF16), jax.ShapeDtypeStruct((d_ff, d), BF16)],
        scratch_shapes=[pltpu.VMEM((d, 2 * tf), BF16)],
        compiler_params=_params("arbitrary", "arbitrary"), name="swiglu_up")(h, wg, wu, wd)


def _proj_stream_body(a_ref, w_ref, x_ref, o_ref):
    o_ref[...] = x_ref[...] + jnp.dot(a_ref[...], w_ref[...], preferred_element_type=F32)


def _proj_stream(a, w, x, name):
    m, k = a.shape
    n = w.shape[1]
    tm = _pick_tile(m, 704, SUBLANES_BF16)
    tn = _pick_tile(n, 256, LANES)
    io_spec = pl.BlockSpec((tm, tn), lambda i, j: (i, j))
    return pl.pallas_call(
        _proj_stream_body, grid=(m // tm, n // tn),
        in_specs=[pl.BlockSpec((tm, k), lambda i, j: (i, 0)), pl.BlockSpec((k, tn), lambda i, j: (0, j)), io_spec],
        out_specs=io_spec,
        out_shape=jax.ShapeDtypeStruct((m, n), F32),
        compiler_params=_params("parallel", "arbitrary"), name=name)(a, w, x)


def _proj_cast_body(a_ref, w_ref, *rest, residual):
    x_ref = rest[0] if residual else None
    o_ref, wb_ref = rest[-2:]
    wb_ref[...] = w_ref[...].astype(BF16)
    acc = jnp.dot(a_ref[...], wb_ref[...], preferred_element_type=F32)
    o_ref[...] = acc + x_ref[...] if residual else acc


def _proj_cast(a, w, x, layer, name):
    m, k = a.shape
    n = w.shape[2]
    residual = x is not None
    tm = _pick_tile(m, 1408, SUBLANES_BF16)
    tn = _pick_tile(n, 256 if residual else 512, LANES)
    io_spec = pl.BlockSpec((tm, tn), lambda i, j: (i, j))
    return pl.pallas_call(
        functools.partial(_proj_cast_body, residual=residual), grid=(m // tm, n // tn),
        in_specs=[pl.BlockSpec((tm, k), lambda i, j: (i, 0)),
                  pl.BlockSpec((None, k, tn), lambda i, j: (layer, 0, j))] + ([io_spec] if residual else []),
        out_specs=io_spec,
        out_shape=jax.ShapeDtypeStruct((m, n), F32),
        scratch_shapes=[pltpu.VMEM((k, tn), BF16)],
        compiler_params=_params("parallel", "arbitrary"), name=name)(*([a, w] + ([x] if residual else [])))


def _qk_body(z_ref, cos_ref, sin_ref, g_ref, o_ref, *, n_q_chunks):
    cos = cos_ref[...]
    sin = sin_ref[...]
    n_chunks = o_ref.shape[1] // HEAD_DIM
    for c in range(n_chunks):
        gi = 0 if c < n_q_chunks else 1
        y = _rms(z_ref[:, c * HEAD_DIM:(c + 1) * HEAD_DIM], g_ref[gi:gi + 1, :])
        o_ref[:, c * HEAD_DIM:(c + 1) * HEAD_DIM] = y * cos + pltpu.roll(y, HEAD_DIM // 2, 1) * sin


def _qk_prep(z, cos, sin, gains, qk_cols):
    m = z.shape[0]
    tr = _pick_tile(m, 344, SUBLANES_F32)
    width = 2 * qk_cols
    return pl.pallas_call(
        functools.partial(_qk_body, n_q_chunks=qk_cols // HEAD_DIM),
        grid=(m // tr,),
        in_specs=[pl.BlockSpec((tr, width), lambda i: (i, 0)),
                  pl.BlockSpec((tr, HEAD_DIM), lambda i: (i, 0)),
                  pl.BlockSpec((tr, HEAD_DIM), lambda i: (i, 0)),
                  pl.BlockSpec((2, HEAD_DIM), lambda i: (0, 0))],
        out_specs=pl.BlockSpec((tr, width), lambda i: (i, 0)),
        out_shape=jax.ShapeDtypeStruct((m, width), F32),
        compiler_params=_params("parallel"), name="qk_norm_rope")(z, cos, sin, gains)


def _kv_out_body(k_ref, v_ref, *rest, n_heads, tr):
    ko_ref, vo_ref, lo_ref, hi_ref = rest[-4:]
    pitch = lo_ref.shape[0] // n_heads
    for src_ref, dst_ref in ((k_ref, ko_ref), (v_ref, vo_ref)):
        for h in range(n_heads):
            lo_ref[h * pitch:h * pitch + tr, :] = src_ref[:, h * V_DIM:h * V_DIM + LANES]
            hi_ref[h * pitch:h * pitch + tr, :] = src_ref[:, h * V_DIM + LANES:(h + 1) * V_DIM]

        def token(t, carry):
            rows = pl.ds(pl.multiple_of(t * n_heads, n_heads), n_heads)
            dst_ref[rows, 0:LANES] = lo_ref[pl.ds(t, n_heads, stride=pitch), :]
            dst_ref[rows, LANES:V_DIM] = hi_ref[pl.ds(t, n_heads, stride=pitch), :]
            return carry

        lax.fori_loop(0, tr, token, 0, unroll=8)


def _kv_out(qk, z, prev, layer, depth, m_rows, n_heads):
    width = n_heads * V_DIM
    tr = _pick_tile(m_rows, 256, SUBLANES_F32)
    pitch = tr + SUBLANES_F32
    shape = jax.ShapeDtypeStruct((depth, m_rows * n_heads, V_DIM), F32)
    out_spec = pl.BlockSpec((None, tr * n_heads, V_DIM), lambda i: (layer, i, 0))
    in_specs = [pl.BlockSpec((tr, width), lambda i: (i, 1)), pl.BlockSpec((tr, width), lambda i: (i, 2))]
    args, aliases = [qk, z], {}
    if prev is not None:
        in_specs += [pl.BlockSpec(memory_space=pl.ANY)] * 2
        args += list(prev)
        aliases = {2: 0, 3: 1}
    return pl.pallas_call(
        functools.partial(_kv_out_body, n_heads=n_heads, tr=tr), grid=(m_rows // tr,),
        in_specs=in_specs, out_specs=[out_spec, out_spec], out_shape=[shape, shape],
        scratch_shapes=[pltpu.VMEM((n_heads * pitch, LANES), F32), pltpu.VMEM((n_heads * pitch, LANES), F32)],
        input_output_aliases=aliases,
        compiler_params=_params("arbitrary"), name="kv_cache_layout")(*args)


def _lambda(lp, lam_init):
    s1 = jnp.sum(lp[0:1, :] * lp[1:2, :], axis=-1, keepdims=True)
    s2 = jnp.sum(lp[2:3, :] * lp[3:4, :], axis=-1, keepdims=True)
    return jnp.exp(s1) - jnp.exp(s2) + lam_init


def _attn_prompt_body(lp_ref, g_ref, q_ref, k_ref, v_ref, o_ref, kb_ref, vb_ref, *, tq, lam_init):
    seq = q_ref.shape[0]
    scale = HEAD_DIM ** -0.5
    lam = _lambda(lp_ref[...], lam_init)
    kb_ref[...] = k_ref[...].astype(BF16)
    vb_ref[...] = v_ref[...].astype(BF16)
    tri = lax.broadcasted_iota(jnp.int32, (tq, tq), 1) <= lax.broadcasted_iota(jnp.int32, (tq, tq), 0)
    nt = (((1,), (1,)), ((), ()))
    for qi in range(seq // tq):
        n0 = qi * tq
        qb = (q_ref[n0:n0 + tq, :] * scale).astype(BF16)
        outs = []
        for mp in range(2):
            lo, hi = mp * HEAD_DIM, (mp + 1) * HEAD_DIM
            s_d = lax.dot_general(qb[:, lo:hi], kb_ref[n0:n0 + tq, lo:hi], nt, preferred_element_type=F32)
            s_d = jnp.where(tri, s_d, -jnp.inf)
            mx = jnp.max(s_d, axis=-1, keepdims=True)
            if qi > 0:
                s_o = lax.dot_general(qb[:, lo:hi], kb_ref[0:n0, lo:hi], nt, preferred_element_type=F32)
                mx = jnp.maximum(mx, jnp.max(s_o, axis=-1, keepdims=True))
            p_d = jnp.exp(s_d - mx)
            l = jnp.sum(p_d, axis=-1, keepdims=True)
            pv = jnp.dot(p_d.astype(BF16), vb_ref[n0:n0 + tq, :], preferred_element_type=F32)
            if qi > 0:
                p_o = jnp.exp(s_o - mx)
                l = l + jnp.sum(p_o, axis=-1, keepdims=True)
                pv = pv + jnp.dot(p_o.astype(BF16), vb_ref[0:n0, :], preferred_element_type=F32)
            outs.append(pv * (1.0 / l))
        o = outs[0] - lam * outs[1]
        o = _rms(o, g_ref[...]) * (1.0 - lam_init)
        o_ref[n0:n0 + tq, :] = o.astype(BF16)


def _attn_prompt(qk, z, lam_params, g_out, batch, seq, n_heads, lam_init, mix_shape):
    k_blk0 = n_heads
    v_blk0 = 2 * n_heads
    tq = _pick_tile(seq, 256, SUBLANES_BF16)
    return pl.pallas_call(
        functools.partial(_attn_prompt_body, tq=tq, lam_init=lam_init),
        grid=(batch, n_heads),
        in_specs=[pl.BlockSpec((4, HEAD_DIM), lambda b, h: (0, 0)),
                  pl.BlockSpec((1, V_DIM), lambda b, h: (0, 0)),
                  pl.BlockSpec((seq, V_DIM), lambda b, h: (b, h)),
                  pl.BlockSpec((seq, V_DIM), lambda b, h: (b, k_blk0 + h)),
                  pl.BlockSpec((seq, V_DIM), lambda b, h: (b, v_blk0 + h))],
        out_specs=pl.BlockSpec((seq, V_DIM), lambda b, h: (b, h)),
        out_shape=jax.ShapeDtypeStruct(mix_shape, BF16),
        scratch_shapes=[pltpu.VMEM((seq, V_DIM), BF16), pltpu.VMEM((seq, V_DIM), BF16)],
        compiler_params=_params("parallel", "parallel"), name="diff_attn_prompt")(
            lam_params, g_out, qk, qk, z)


def _diag_column(x, valid):
    n = x.shape[1]
    r = jnp.sum(jnp.where(valid, x, 0.0), axis=0, keepdims=True)
    eye = lax.broadcasted_iota(jnp.int32, (n, n), 0) == lax.broadcasted_iota(jnp.int32, (n, n), 1)
    return jnp.sum(jnp.where(eye, jnp.broadcast_to(r, (n, n)), 0.0), axis=1, keepdims=True)


def _attn_sample_body(pt_ref, lp_ref, g_ref, qt_ref, *rest, n_pages_step, n_heads, t_new, lam_init):
    k_refs = rest[:n_pages_step]
    v_refs = rest[n_pages_step:2 * n_pages_step]
    kn_ref, vn_ref, o_ref, m_ref, l_ref, acc_ref = rest[2 * n_pages_step:]
    j = pl.program_id(1)
    qt = qt_ref[...]
    n_cols = qt.shape[1]
    col_head = lax.broadcasted_iota(jnp.int32, (n_heads, n_cols), 1) // (2 * t_new)
    valid = col_head == lax.broadcasted_iota(jnp.int32, (n_heads, n_cols), 0)

    @pl.when(j == 0)
    def _():
        m_ref[...] = jnp.full(m_ref.shape, -jnp.inf, F32)
        l_ref[...] = jnp.zeros(l_ref.shape, F32)
        acc_ref[...] = jnp.zeros(acc_ref.shape, F32)

    def update(kb, vb, causal):
        s = jnp.dot(kb, qt, preferred_element_type=F32)
        s = s.reshape(-1, n_heads, n_cols)
        keep = valid[None]
        if causal:
            key_t = lax.broadcasted_iota(jnp.int32, s.shape, 0)
            query_t = lax.broadcasted_iota(jnp.int32, s.shape, 2) % t_new
            keep = keep & (key_t <= query_t)
        s = jnp.where(keep, s, -jnp.inf)
        m_old = m_ref[...]
        m_new = jnp.maximum(m_old, jnp.max(s, axis=0))
        m_safe = jnp.where(valid, m_new, 0.0)
        alpha = jnp.where(valid, jnp.exp(m_old - m_safe), 0.0)
        p = jnp.exp(s - m_safe[None])
        l_ref[...] = alpha * l_ref[...] + jnp.sum(p, axis=0)
        m_ref[...] = m_new
        pv = lax.dot_general(p.reshape(-1, n_cols).astype(BF16), vb, (((0,), (0,)), ((), ())),
                             preferred_element_type=F32)
        acc_ref[...] = _diag_column(alpha, valid) * acc_ref[...] + pv

    update(jnp.concatenate([r[...].astype(BF16) for r in k_refs], axis=0),
           jnp.concatenate([r[...].astype(BF16) for r in v_refs], axis=0), causal=False)

    @pl.when(j == pl.num_programs(1) - 1)
    def _():
        update(kn_ref[...].astype(BF16), vn_ref[...].astype(BF16), causal=True)
        lam = _lambda(lp_ref[...], lam_init)
        out = acc_ref[...] * (1.0 / _diag_column(l_ref[...], valid))
        for h in range(n_heads):
            r0 = 2 * h * t_new
            o = out[r0:r0 + t_new, :] - lam * out[r0 + t_new:r0 + 2 * t_new, :]
            o_ref[:, h * V_DIM:(h + 1) * V_DIM] = _rms(o, g_ref[...]) * (1.0 - lam_init)


def _attn_sample(page_table, q_t, cache_k, cache_v, k_new, v_new, lam_params, g_out, layer, n_heads, t_new,
                 lam_init):
    dec_b, n_pages = page_table.shape
    page_rows = cache_k.shape[2]
    n_cols = q_t.shape[2]
    nps = _pick_tile(n_pages, PAGES_PER_STEP, 1)

    def page_spec(i):
        return pl.BlockSpec((None, None, page_rows, V_DIM), lambda b, j, pt: (layer, pt[b, j * nps + i], 0, 0))

    new_spec = pl.BlockSpec((None, t_new * n_heads, V_DIM), lambda b, j, pt: (b, 0, 0))
    grid_spec = pltpu.PrefetchScalarGridSpec(
        num_scalar_prefetch=1, grid=(dec_b, n_pages // nps),
        in_specs=([pl.BlockSpec((4, HEAD_DIM), lambda b, j, pt: (0, 0)),
                   pl.BlockSpec((1, V_DIM), lambda b, j, pt: (0, 0)),
                   pl.BlockSpec((None, 2 * HEAD_DIM, n_cols), lambda b, j, pt: (b, 0, 0))]
                  + [page_spec(i) for i in range(nps)] + [page_spec(i) for i in range(nps)]
                  + [new_spec, new_spec]),
        out_specs=pl.BlockSpec((None, t_new, n_heads * V_DIM), lambda b, j, pt: (b, 0, 0)),
        scratch_shapes=[pltpu.VMEM((n_heads, n_cols), F32), pltpu.VMEM((n_heads, n_cols), F32),
                        pltpu.VMEM((n_cols, V_DIM), F32)])
    return pl.pallas_call(
        functools.partial(_attn_sample_body, n_pages_step=nps, n_heads=n_heads, t_new=t_new, lam_init=lam_init),
        grid_spec=grid_spec,
        out_shape=jax.ShapeDtypeStruct((dec_b, t_new, n_heads * V_DIM), F32),
        compiler_params=_params("parallel", "arbitrary"), name="diff_attn_sample")(
            page_table, lam_params, g_out, q_t, *([cache_k] * nps), *([cache_v] * nps), k_new, v_new)


def _conv_body(*refs, tt, has_state, row_chunk):
    sb_ref, sc_ref, sh_ref, cv_ref, cg_ref = refs[:5]
    ss_ref, cs_ref = refs[5:7] if has_state else (None, None)
    sw_ref, cw_ref, cb_ref, lg_ref, lb_ref = refs[7:12] if has_state else refs[5:10]
    y_ref, so_ref, co_ref, ue_ref, ce_ref, acc_ref = refs[-6:]
    t = pl.program_id(1)
    width = ue_ref.shape[1]

    @pl.when(t == 0)
    def _():
        ue_ref[0:SHORT_PAD, :] = jnp.zeros((SHORT_PAD, width), F32)
        ce_ref[0:CONF_PAD, :] = jnp.zeros((CONF_PAD, width), F32)
        if has_state:
            ue_ref[SHORT_PAD - (SHORT_W - 1):SHORT_PAD, :] = ss_ref[0]
            ce_ref[CONF_PAD - (CONF_W - 1):CONF_PAD, :] = cs_ref[0]

    ue_ref[SHORT_PAD:SHORT_PAD + tt, :] = sc_ref[...] * sh_ref[...]
    ce_ref[CONF_PAD:CONF_PAD + tt, :] = cv_ref[...] * jax.nn.sigmoid(cg_ref[...])

    conv = jnp.zeros((tt, width), F32)
    for jt in range(SHORT_W):
        off = SHORT_PAD - (SHORT_W - 1) + jt
        conv = conv + sw_ref[jt:jt + 1, :] * ue_ref[off:off + tt, :]
    y_ref[:, :width] = (sb_ref[...] * conv).astype(y_ref.dtype)

    base = CONF_PAD - (CONF_W - 1)
    for r0 in range(0, tt, row_chunk):
        for c0 in range(0, width, LANES):
            acc = jnp.zeros((row_chunk, LANES), F32)
            for b in range(SUBLANES_F32):
                n_a = (CONF_W - 1 - b) // SUBLANES_F32 + 1
                rows = row_chunk + SUBLANES_F32 * (n_a - 1)
                win = ce_ref[r0 + base + b:r0 + base + b + rows, c0:c0 + LANES]
                for a in range(n_a):
                    jt = SUBLANES_F32 * a + b
                    acc = acc + cw_ref[jt:jt + 1, c0:c0 + LANES] * win[SUBLANES_F32 * a:SUBLANES_F32 * a + row_chunk, :]
            acc_ref[r0:r0 + row_chunk, c0:c0 + LANES] = acc
    y = acc_ref[...] + cb_ref[...]
    mu = jnp.mean(y, axis=-1, keepdims=True)
    var = jnp.mean(jnp.square(y - mu), axis=-1, keepdims=True)
    y = (y - mu) * lax.rsqrt(var + LN_EPS) * lg_ref[...] + lb_ref[...]
    y_ref[:, width:] = (y * jax.nn.sigmoid(y)).astype(y_ref.dtype)

    @pl.when(t == pl.num_programs(1) - 1)
    def _():
        so_ref[0] = ue_ref[SHORT_PAD + tt - (SHORT_W - 1):SHORT_PAD + tt, :]
        co_ref[0] = ce_ref[CONF_PAD + tt - (CONF_W - 1):CONF_PAD + tt, :]

    @pl.when(t < pl.num_programs(1) - 1)
    def _():
        ue_ref[0:SHORT_PAD, :] = ue_ref[tt:tt + SHORT_PAD, :]
        ce_ref[0:CONF_PAD, :] = ce_ref[tt:tt + CONF_PAD, :]


def _convs(z, row0, n_seq, seq, width, col_blk0, weights, states, mix, name):
    sw, cw, cb, lg, lb = weights
    tt = _pick_tile(seq, 256, SUBLANES_F32)
    nt = seq // tt
    rb0 = row0 // tt
    row_chunk = _pick_tile(tt, 64, SUBLANES_F32)
    assert nt == 1 or tt >= CONF_PAD
    has_state = states is not None

    def zspec(k):
        return pl.BlockSpec((tt, width), lambda b, t: (rb0 + b * nt + t, col_blk0 + k))

    def full(a):
        return pl.BlockSpec(a.shape, lambda b, t: (0,) * a.ndim)

    in_specs = [zspec(k) for k in range(5)]
    args = [z] * 5
    if has_state:
        in_specs += [pl.BlockSpec((1, SHORT_W - 1, width), lambda b, t: (b, 0, 0)),
                     pl.BlockSpec((1, CONF_W - 1, width), lambda b, t: (b, 0, 0))]
        args += list(states)
    in_specs += [full(a) for a in (sw, cw, cb, lg, lb)]
    args += [sw, cw, cb, lg, lb]
    if mix is None:
        y_shape = jax.ShapeDtypeStruct((n_seq * seq, 2 * width), F32)
        y_col_blk, aliases = 0, {}
    else:
        y_shape = jax.ShapeDtypeStruct(mix.shape, mix.dtype)
        y_col_blk, aliases = mix.shape[1] // (2 * width) - 1, {len(args): 0}
        in_specs.append(pl.BlockSpec(memory_space=pl.ANY))
        args.append(mix)
    return pl.pallas_call(
        functools.partial(_conv_body, tt=tt, has_state=has_state, row_chunk=row_chunk),
        grid=(n_seq, nt), in_specs=in_specs,
        out_specs=[pl.BlockSpec((tt, 2 * width), lambda b, t: (b * nt + t, y_col_blk)),
                   pl.BlockSpec((1, SHORT_W - 1, width), lambda b, t: (b, 0, 0)),
                   pl.BlockSpec((1, CONF_W - 1, width), lambda b, t: (b, 0, 0))],
        out_shape=[y_shape,
                   jax.ShapeDtypeStruct((n_seq, SHORT_W - 1, width), F32),
                   jax.ShapeDtypeStruct((n_seq, CONF_W - 1, width), F32)],
        scratch_shapes=[pltpu.VMEM((SHORT_PAD + tt, width), F32), pltpu.VMEM((CONF_PAD + tt, width), F32),
                        pltpu.VMEM((tt, width), F32)],
        input_output_aliases=aliases,
        compiler_params=_params("parallel", "arbitrary"), name=name)(*args)


def _rope_tables(positions):
    half = HEAD_DIM // 2
    inv_freq = jnp.power(ROPE_THETA, -jnp.arange(half, dtype=F32) * 2.0 / HEAD_DIM)
    ang = positions.astype(F32)[:, None] * inv_freq[None, :]
    cos, sin = jnp.cos(ang), jnp.sin(ang)
    return jnp.concatenate([cos, cos], axis=-1), jnp.concatenate([-sin, sin], axis=-1)


def kernel(x_prompt, x_sample, cache_k, cache_v, state_short, state_conf, page_table, norm_ffn1, ffn1_w_gate, ffn1_w_up, ffn1_w_down, norm_mix, w_in, q_norm, k_norm, lambda_q1, lambda_k1, lambda_q2, lambda_k2, attn_out_norm, short_w, conf_w, conf_b, conf_ln_g, conf_ln_b, w_out, norm_ffn2, ffn2_w_gate, ffn2_w_up, ffn2_w_down):
    batch, seq, d_model = x_prompt.shape
    dec_b, dec_s, _ = x_sample.shape
    depth, n_pool, page, n_heads, _ = cache_k.shape
    past_len = page_table.shape[1] * page
    qk_cols = n_heads * 2 * HEAD_DIM
    d_short = short_w.shape[2]
    m_prompt, m_sample = batch * seq, dec_b * dec_s
    width = n_heads * V_DIM

    cache_k2 = cache_k.reshape(depth, n_pool, page * n_heads, V_DIM)
    cache_v2 = cache_v.reshape(depth, n_pool, page * n_heads, V_DIM)

    positions = jnp.concatenate([jnp.tile(jnp.arange(seq), batch), jnp.tile(past_len + jnp.arange(dec_s), dec_b)])
    cos, sin = _rope_tables(positions)
    n_maps = 2 * n_heads
    eye = jnp.eye(2, dtype=F32)

    x = jnp.concatenate([x_prompt.reshape(m_prompt, d_model), x_sample.reshape(m_sample, d_model)], axis=0)
    row = lambda a, l: a[l].reshape(1, -1)

    outs = {k: [] for k in ("sp", "cp", "ks", "vs", "ss", "cs")}
    kv_prompt = None
    for l in range(depth):
        lam_init = 0.8 - 0.6 * math.exp(-0.3 * l)
        lam_params = jnp.stack([lambda_q1[l], lambda_k1[l], lambda_q2[l], lambda_k2[l]])
        g_out = row(attn_out_norm, l)
        conv_w = (short_w[l], conf_w[l], row(conf_b, l), row(conf_ln_g, l), row(conf_ln_b, l))

        act, wd_b = _ffn_up(_norm(x, row(norm_ffn1, l)), ffn1_w_gate, ffn1_w_up, ffn1_w_down, l)
        x = _proj_stream(act, wd_b, x, "swiglu_down")

        z = _proj_cast(_norm(x, row(norm_mix, l)), w_in, None, l, "w_in_proj")
        qk = _qk_prep(z, cos, sin, jnp.stack([q_norm[l], k_norm[l]]), qk_cols)

        mix = _attn_prompt(qk, z, lam_params, g_out, batch, seq, n_heads, lam_init, x.shape)

        q_s = qk[m_prompt:, :qk_cols].reshape(dec_b, dec_s, n_heads, 2, HEAD_DIM) * (HEAD_DIM ** -0.5)
        q_t = jnp.einsum("bthmd,mn->bmdhnt", q_s, eye).reshape(dec_b, 2 * HEAD_DIM, n_maps * dec_s).astype(BF16)
        k_s = qk[m_prompt:, qk_cols:].reshape(dec_b, dec_s * n_heads, V_DIM)
        v_s = z[m_prompt:, 2 * qk_cols:2 * qk_cols + width].reshape(dec_b, dec_s * n_heads, V_DIM)
        attn_s = _attn_sample(page_table, q_t, cache_k2, cache_v2, k_s, v_s,
                              lam_params, g_out, l, n_heads, dec_s, lam_init)

        conv_blk0 = (2 * qk_cols + width) // d_short
        mix, sp, cp = _convs(z, 0, batch, seq, d_short, conv_blk0, conv_w, None, mix, "convs_prompt")
        y_s, ss, cs = _convs(z, m_prompt, dec_b, dec_s, d_short, conv_blk0, conv_w,
                             (state_short[l], state_conf[l]), None, "convs_sample")
        mix_s = jnp.concatenate([attn_s.reshape(m_sample, width), y_s], axis=1).astype(BF16)
        mix = lax.dynamic_update_slice(mix, mix_s, (m_prompt, 0))

        x = _proj_cast(mix, w_out, x, l, "w_out_proj")
        act, wd_b = _ffn_up(_norm(x, row(norm_ffn2, l)), ffn2_w_gate, ffn2_w_up, ffn2_w_down, l)
        x = _proj_stream(act, wd_b, x, "swiglu_down")

        kv_prompt = _kv_out(qk, z, kv_prompt, l, depth, m_prompt, n_heads)
        outs["ks"].append(k_s.reshape(dec_b, dec_s, n_heads, V_DIM))
        outs["vs"].append(v_s.reshape(dec_b, dec_s, n_heads, V_DIM))
        outs["sp"].append(sp)
        outs["cp"].append(cp)
        outs["ss"].append(ss)
        outs["cs"].append(cs)

    st = {k: jnp.stack(v) for k, v in outs.items()}
    k_prompt, v_prompt = (a.reshape(depth, batch, seq, n_heads, V_DIM) for a in kv_prompt)
    return (x[:m_prompt].reshape(batch, seq, d_model), x[m_prompt:].reshape(dec_b, dec_s, d_model),
            k_prompt, v_prompt, st["sp"], st["cp"], st["ks"], st["vs"], st["ss"], st["cs"])
```

```python
import functools
import math

import jax
import jax.numpy as jnp
from jax import lax
from jax.experimental import pallas as pl
from jax.experimental.pallas import tpu as pltpu

F32 = jnp.float32
BF16 = jnp.bfloat16

HEAD_DIM = 128
V_DIM = 2 * HEAD_DIM
SHORT_W = 3
CONF_W = 31
ROPE_THETA = 10000.0
NORM_EPS = 1e-6
LN_EPS = 1e-5

LANES = 128
SUBLANES_F32 = 8
SUBLANES_BF16 = 16
VMEM_LIMIT_BYTES = 56 * 1024 * 1024

SHORT_PAD = 8
CONF_PAD = 32
PAGES_PER_STEP = 8


def _pick_tile(n, cap, align):
    best = None
    for d in range(align, min(n, cap) + 1, align):
        if n % d == 0:
            best = d
    if best is None:
        raise ValueError(f"no tile for n={n} cap={cap} align={align}")
    return best


def _params(*sem):
    return pltpu.CompilerParams(dimension_semantics=sem, vmem_limit_bytes=VMEM_LIMIT_BYTES)


def _rms(x, g):
    ms = jnp.mean(x * x, axis=-1, keepdims=True)
    return x * lax.rsqrt(ms + NORM_EPS) * g


def _norm_body(x_ref, g_ref, h_ref):
    h_ref[...] = _rms(x_ref[...], g_ref[...]).astype(BF16)


def _row_spec(tr, d):
    return pl.BlockSpec((tr, d), lambda i: (i, 0))


def _norm(x, g):
    m, d = x.shape
    tr = _pick_tile(m, 704, SUBLANES_BF16)
    return pl.pallas_call(
        _norm_body, grid=(m // tr,),
        in_specs=[_row_spec(tr, d), pl.BlockSpec((1, d), lambda i: (0, 0))],
        out_specs=_row_spec(tr, d),
        out_shape=jax.ShapeDtypeStruct((m, d), BF16),
        compiler_params=_params("parallel"), name="rmsnorm")(x, g)


def _ffn_up_body(h_ref, wg_ref, wu_ref, wd_ref, a_ref, wdb_ref, w_ref):
    tf = wg_ref.shape[1]
    w_ref[:, :tf] = wg_ref[...].astype(BF16)
    w_ref[:, tf:] = wu_ref[...].astype(BF16)
    tm = h_ref.shape[0]
    n_split = 2 if tm % (2 * SUBLANES_BF16) == 0 else 1
    rows = tm // n_split
    for r0 in range(0, tm, rows):
        r = jnp.dot(h_ref[r0:r0 + rows, :], w_ref[...], preferred_element_type=F32)
        g, u = r[:, :tf], r[:, tf:]
        a_ref[r0:r0 + rows, :] = (g * jax.nn.sigmoid(g) * u * 0.5).astype(BF16)
    wdb_ref[...] = wd_ref[...].astype(BF16)


def _ffn_up(h, wg, wu, wd, layer):
    m, d = h.shape
    d_ff = wg.shape[2]
    tm = _pick_tile(m, 1408, SUBLANES_BF16)
    tf = _pick_tile(d_ff, 256, LANES)
    nf = d_ff // tf
    n_cast = max(c for c in (1, 2, 4) if m // tm >= c and d % (c * LANES) == 0)
    tc = d // n_cast

    def wd_blk(i, f):
        live = i < n_cast
        return jnp.where(live, f, nf - 1), jnp.where(live, i, n_cast - 1)

    return pl.pallas_call(
        _ffn_up_body, grid=(m // tm, nf),
        in_specs=[pl.BlockSpec((tm, d), lambda i, f: (i, 0)),
                  pl.BlockSpec((None, d, tf), lambda i, f: (layer, 0, f)),
                  pl.BlockSpec((None, d, tf), lambda i, f: (layer, 0, f)),
                  pl.BlockSpec((None, tf, tc), lambda i, f: (layer,) + wd_blk(i, f))],
        out_specs=[pl.BlockSpec((tm, tf), lambda i, f: (i, f)),
                   pl.BlockSpec((tf, tc), wd_blk)],
        out_shape=[jax.ShapeDtypeStruct((m, d_ff), BF16), jax.ShapeDtypeStruct((d_ff, d), BF16)],
        scratch_shapes=[pltpu.VMEM((d, 2 * tf), BF16)],
        compiler_params=_params("arbitrary", "arbitrary"), name="swiglu_up")(h, wg, wu, wd)


def _proj_stream_body(a_ref, w_ref, x_ref, o_ref):
    o_ref[...] = x_ref[...] + jnp.dot(a_ref[...], w_ref[...], preferred_element_type=F32)


def _proj_stream(a, w, x, name):
    m, k = a.shape
    n = w.shape[1]
    tm = _pick_tile(m, 704, SUBLANES_BF16)
    tn = _pick_tile(n, 256, LANES)
    io_spec = pl.BlockSpec((tm, tn), lambda i, j: (i, j))
    return pl.pallas_call(
        _proj_stream_body, grid=(m // tm, n // tn),
        in_specs=[pl.BlockSpec((tm, k), lambda i, j: (i, 0)), pl.BlockSpec((k, tn), lambda i, j: (0, j)), io_spec],
        out_specs=io_spec,
        out_shape=jax.ShapeDtypeStruct((m, n), F32),
        compiler_params=_params("parallel", "arbitrary"), name=name)(a, w, x)


def _proj_cast_body(a_ref, w_ref, *rest, residual):
    x_ref = rest[0] if residual else None
    o_ref, wb_ref = rest[-2:]
    wb_ref[...] = w_ref[...].astype(BF16)
    tm = a_ref.shape[0]
    n_split = 2 if tm % (2 * SUBLANES_BF16) == 0 else 1
    rows = tm // n_split
    for r0 in range(0, tm, rows):
        acc = jnp.dot(a_ref[r0:r0 + rows, :], wb_ref[...], preferred_element_type=F32)
        o_ref[r0:r0 + rows, :] = acc + x_ref[r0:r0 + rows, :] if residual else acc


def _proj_cast(a, w, x, layer, name):
    m, k = a.shape
    n = w.shape[2]
    residual = x is not None
    tm = _pick_tile(m, 1408, SUBLANES_BF16)
    tn = _pick_tile(n, 256 if residual else 512, LANES)
    io_spec = pl.BlockSpec((tm, tn), lambda i, j: (i, j))
    return pl.pallas_call(
        functools.partial(_proj_cast_body, residual=residual), grid=(m // tm, n // tn),
        in_specs=[pl.BlockSpec((tm, k), lambda i, j: (i, 0)),
                  pl.BlockSpec((None, k, tn), lambda i, j: (layer, 0, j))] + ([io_spec] if residual else []),
        out_specs=io_spec,
        out_shape=jax.ShapeDtypeStruct((m, n), F32),
        scratch_shapes=[pltpu.VMEM((k, tn), BF16)],
        compiler_params=_params("parallel", "arbitrary"), name=name)(*([a, w] + ([x] if residual else [])))


def _qk_body(z_ref, cos_ref, sin_ref, g_ref, o_ref, *, n_q_chunks):
    cos = cos_ref[...]
    sin = sin_ref[...]
    n_chunks = o_ref.shape[1] // HEAD_DIM
    ones = jnp.ones((HEAD_DIM, HEAD_DIM), BF16)
    for c in range(n_chunks):
        gi = 0 if c < n_q_chunks else 1
        x = z_ref[:, c * HEAD_DIM:(c + 1) * HEAD_DIM]
        sq = x * x
        hi = sq.astype(BF16)
        lo = (sq - hi.astype(F32)).astype(BF16)
        ssq = (jnp.dot(hi, ones, preferred_element_type=F32) + jnp.dot(lo, ones, preferred_element_type=F32))
        y = x * lax.rsqrt(ssq * (1.0 / HEAD_DIM) + NORM_EPS) * g_ref[gi:gi + 1, :]
        o_ref[:, c * HEAD_DIM:(c + 1) * HEAD_DIM] = y * cos + pltpu.roll(y, HEAD_DIM // 2, 1) * sin


def _qk_prep(z, cos, sin, gains, qk_cols):
    m = z.shape[0]
    tr = _pick_tile(m, 704, SUBLANES_BF16)
    width = 2 * qk_cols
    return pl.pallas_call(
        functools.partial(_qk_body, n_q_chunks=qk_cols // HEAD_DIM),
        grid=(m // tr,),
        in_specs=[pl.BlockSpec((tr, width), lambda i: (i, 0)),
                  pl.BlockSpec((tr, HEAD_DIM), lambda i: (i, 0)),
                  pl.BlockSpec((tr, HEAD_DIM), lambda i: (i, 0)),
                  pl.BlockSpec((2, HEAD_DIM), lambda i: (0, 0))],
        out_specs=pl.BlockSpec((tr, width), lambda i: (i, 0)),
        out_shape=jax.ShapeDtypeStruct((m, width), F32),
        compiler_params=_params("parallel"), name="qk_norm_rope")(z, cos, sin, gains)


def _kv_out_body(k_ref, v_ref, *rest, n_heads, tr):
    ko_ref, vo_ref, lo_ref, hi_ref = rest[-4:]
    pitch = lo_ref.shape[0] // n_heads
    for src_ref, dst_ref in ((k_ref, ko_ref), (v_ref, vo_ref)):
        for h in range(n_heads):
            lo_ref[h * pitch:h * pitch + tr, :] = src_ref[:, h * V_DIM:h * V_DIM + LANES]
            hi_ref[h * pitch:h * pitch + tr, :] = src_ref[:, h * V_DIM + LANES:(h + 1) * V_DIM]

        def token(t, carry):
            rows = pl.ds(pl.multiple_of(t * n_heads, n_heads), n_heads)
            dst_ref[rows, 0:LANES] = lo_ref[pl.ds(t, n_heads, stride=pitch), :]
            dst_ref[rows, LANES:V_DIM] = hi_ref[pl.ds(t, n_heads, stride=pitch), :]
            return carry

        lax.fori_loop(0, tr, token, 0, unroll=8)


def _kv_out(qk, z, prev, layer, depth, m_rows, n_heads):
    width = n_heads * V_DIM
    tr = _pick_tile(m_rows, 256, SUBLANES_F32)
    pitch = tr + SUBLANES_F32
    shape = jax.ShapeDtypeStruct((depth, m_rows * n_heads, V_DIM), F32)
    out_spec = pl.BlockSpec((None, tr * n_heads, V_DIM), lambda i: (layer, i, 0))
    in_specs = [pl.BlockSpec((tr, width), lambda i: (i, 1)), pl.BlockSpec((tr, width), lambda i: (i, 2))]
    args, aliases = [qk, z], {}
    if prev is not None:
        in_specs += [pl.BlockSpec(memory_space=pl.ANY)] * 2
        args += list(prev)
        aliases = {2: 0, 3: 1}
    return pl.pallas_call(
        functools.partial(_kv_out_body, n_heads=n_heads, tr=tr), grid=(m_rows // tr,),
        in_specs=in_specs, out_specs=[out_spec, out_spec], out_shape=[shape, shape],
        scratch_shapes=[pltpu.VMEM((n_heads * pitch, LANES), F32), pltpu.VMEM((n_heads * pitch, LANES), F32)],
        input_output_aliases=aliases,
        compiler_params=_params("arbitrary"), name="kv_cache_layout")(*args)


def _lambda(lp, lam_init):
    s1 = jnp.sum(lp[0:1, :] * lp[1:2, :], axis=-1, keepdims=True)
    s2 = jnp.sum(lp[2:3, :] * lp[3:4, :], axis=-1, keepdims=True)
    return jnp.exp(s1) - jnp.exp(s2) + lam_init


def _attn_prompt_body(lp_ref, g_ref, q_ref, k_ref, v_ref, o_ref, kb_ref, vb_ref, *, tq, lam_init):
    seq = q_ref.shape[0]
    scale = HEAD_DIM ** -0.5
    lam = _lambda(lp_ref[...], lam_init)
    kb_ref[...] = k_ref[...].astype(BF16)
    vb_ref[...] = v_ref[...].astype(BF16)
    tri = lax.broadcasted_iota(jnp.int32, (tq, tq), 1) <= lax.broadcasted_iota(jnp.int32, (tq, tq), 0)
    nt = (((1,), (1,)), ((), ()))
    for qi in range(seq // tq):
        n0 = qi * tq
        qb = (q_ref[n0:n0 + tq, :] * scale).astype(BF16)
        outs = []
        for mp in range(2):
            lo, hi = mp * HEAD_DIM, (mp + 1) * HEAD_DIM
            s_d = lax.dot_general(qb[:, lo:hi], kb_ref[n0:n0 + tq, lo:hi], nt, preferred_element_type=F32)
            s_d = jnp.where(tri, s_d, -jnp.inf)
            mx = jnp.max(s_d, axis=-1, keepdims=True)
            if qi > 0:
                s_o = lax.dot_general(qb[:, lo:hi], kb_ref[0:n0, lo:hi], nt, preferred_element_type=F32)
                mx = jnp.maximum(mx, jnp.max(s_o, axis=-1, keepdims=True))
            p_d = jnp.exp(s_d - mx)
            l = jnp.sum(p_d, axis=-1, keepdims=True)
            pv = jnp.dot(p_d.astype(BF16), vb_ref[n0:n0 + tq, :], preferred_element_type=F32)
            if qi > 0:
                p_o = jnp.exp(s_o - mx)
                l = l + jnp.sum(p_o, axis=-1, keepdims=True)
                pv = pv + jnp.dot(p_o.astype(BF16), vb_ref[0:n0, :], preferred_element_type=F32)
            outs.append(pv * (1.0 / l))
        o = outs[0] - lam * outs[1]
        o = _rms(o, g_ref[...]) * (1.0 - lam_init)
        o_ref[n0:n0 + tq, :] = o.astype(BF16)


def _attn_prompt(qk, z, lam_params, g_out, batch, seq, n_heads, lam_init, mix_shape):
    k_blk0 = n_heads
    v_blk0 = 2 * n_heads
    tq = _pick_tile(seq, 256, SUBLANES_BF16)
    return pl.pallas_call(
        functools.partial(_attn_prompt_body, tq=tq, lam_init=lam_init),
        grid=(batch, n_heads),
        in_specs=[pl.BlockSpec((4, HEAD_DIM), lambda b, h: (0, 0)),
                  pl.BlockSpec((1, V_DIM), lambda b, h: (0, 0)),
                  pl.BlockSpec((seq, V_DIM), lambda b, h: (b, h)),
                  pl.BlockSpec((seq, V_DIM), lambda b, h: (b, k_blk0 + h)),
                  pl.BlockSpec((seq, V_DIM), lambda b, h: (b, v_blk0 + h))],
        out_specs=pl.BlockSpec((seq, V_DIM), lambda b, h: (b, h)),
        out_shape=jax.ShapeDtypeStruct(mix_shape, BF16),
        scratch_shapes=[pltpu.VMEM((seq, V_DIM), BF16), pltpu.VMEM((seq, V_DIM), BF16)],
        compiler_params=_params("parallel", "parallel"), name="diff_attn_prompt")(
            lam_params, g_out, qk, qk, z)


def _diag_column(x, valid):
    n = x.shape[1]
    r = jnp.sum(jnp.where(valid, x, 0.0), axis=0, keepdims=True)
    eye = lax.broadcasted_iota(jnp.int32, (n, n), 0) == lax.broadcasted_iota(jnp.int32, (n, n), 1)
    return jnp.sum(jnp.where(eye, jnp.broadcast_to(r, (n, n)), 0.0), axis=1, keepdims=True)


def _attn_sample_body(pt_ref, lp_ref, g_ref, qt_ref, *rest, n_pages_step, n_heads, t_new, lam_init):
    k_refs = rest[:n_pages_step]
    v_refs = rest[n_pages_step:2 * n_pages_step]
    kn_ref, vn_ref, o_ref, m_ref, l_ref, acc_ref = rest[2 * n_pages_step:]
    j = pl.program_id(1)
    qt = qt_ref[...]
    n_cols = qt.shape[1]
    col_head = lax.broadcasted_iota(jnp.int32, (n_heads, n_cols), 1) // (2 * t_new)
    valid = col_head == lax.broadcasted_iota(jnp.int32, (n_heads, n_cols), 0)

    @pl.when(j == 0)
    def _():
        m_ref[...] = jnp.full(m_ref.shape, -jnp.inf, F32)
        l_ref[...] = jnp.zeros(l_ref.shape, F32)
        acc_ref[...] = jnp.zeros(acc_ref.shape, F32)

    def update(kb, vb, causal):
        s = jnp.dot(kb, qt, preferred_element_type=F32)
        s = s.reshape(-1, n_heads, n_cols)
        keep = valid[None]
        if causal:
            key_t = lax.broadcasted_iota(jnp.int32, s.shape, 0)
            query_t = lax.broadcasted_iota(jnp.int32, s.shape, 2) % t_new
            keep = keep & (key_t <= query_t)
        s = jnp.where(keep, s, -jnp.inf)
        m_old = m_ref[...]
        m_new = jnp.maximum(m_old, jnp.max(s, axis=0))
        m_safe = jnp.where(valid, m_new, 0.0)
        alpha = jnp.where(valid, jnp.exp(m_old - m_safe), 0.0)
        p = jnp.exp(s - m_safe[None])
        l_ref[...] = alpha * l_ref[...] + jnp.sum(p, axis=0)
        m_ref[...] = m_new
        pv = lax.dot_general(p.reshape(-1, n_cols).astype(BF16), vb, (((0,), (0,)), ((), ())),
                             preferred_element_type=F32)
        acc_ref[...] = _diag_column(alpha, valid) * acc_ref[...] + pv

    update(jnp.concatenate([r[...].astype(BF16) for r in k_refs], axis=0),
           jnp.concatenate([r[...].astype(BF16) for r in v_refs], axis=0), causal=False)

    @pl.when(j == pl.num_programs(1) - 1)
    def _():
        update(kn_ref[...].astype(BF16), vn_ref[...].astype(BF16), causal=True)
        lam = _lambda(lp_ref[...], lam_init)
        out = acc_ref[...] * (1.0 / _diag_column(l_ref[...], valid))
        for h in range(n_heads):
            r0 = 2 * h * t_new
            o = out[r0:r0 + t_new, :] - lam * out[r0 + t_new:r0 + 2 * t_new, :]
            o_ref[:, h * V_DIM:(h + 1) * V_DIM] = _rms(o, g_ref[...]) * (1.0 - lam_init)


def _attn_sample(page_table, q_t, cache_k, cache_v, k_new, v_new, lam_params, g_out, layer, n_heads, t_new,
                 lam_init):
    dec_b, n_pages = page_table.shape
    page_rows = cache_k.shape[2]
    n_cols = q_t.shape[2]
    nps = _pick_tile(n_pages, PAGES_PER_STEP, 1)

    def page_spec(i):
        return pl.BlockSpec((None, None, page_rows, V_DIM), lambda b, j, pt: (layer, pt[b, j * nps + i], 0, 0))

    new_spec = pl.BlockSpec((None, t_new * n_heads, V_DIM), lambda b, j, pt: (b, 0, 0))
    grid_spec = pltpu.PrefetchScalarGridSpec(
        num_scalar_prefetch=1, grid=(dec_b, n_pages // nps),
        in_specs=([pl.BlockSpec((4, HEAD_DIM), lambda b, j, pt: (0, 0)),
                   pl.BlockSpec((1, V_DIM), lambda b, j, pt: (0, 0)),
                   pl.BlockSpec((None, 2 * HEAD_DIM, n_cols), lambda b, j, pt: (b, 0, 0))]
                  + [page_spec(i) for i in range(nps)] + [page_spec(i) for i in range(nps)]
                  + [new_spec, new_spec]),
        out_specs=pl.BlockSpec((None, t_new, n_heads * V_DIM), lambda b, j, pt: (b, 0, 0)),
        scratch_shapes=[pltpu.VMEM((n_heads, n_cols), F32), pltpu.VMEM((n_heads, n_cols), F32),
                        pltpu.VMEM((n_cols, V_DIM), F32)])
    return pl.pallas_call(
        functools.partial(_attn_sample_body, n_pages_step=nps, n_heads=n_heads, t_new=t_new, lam_init=lam_init),
        grid_spec=grid_spec,
        out_shape=jax.ShapeDtypeStruct((dec_b, t_new, n_heads * V_DIM), F32),
        compiler_params=_params("parallel", "arbitrary"), name="diff_attn_sample")(
            page_table, lam_params, g_out, q_t, *([cache_k] * nps), *([cache_v] * nps), k_new, v_new)


def _conv_body(*refs, tt, has_state, row_chunk):
    sb_ref, sc_ref, sh_ref, cv_ref, cg_ref = refs[:5]
    ss_ref, cs_ref = refs[5:7] if has_state else (None, None)
    sw_ref, cw_ref, cb_ref, lg_ref, lb_ref = refs[7:12] if has_state else refs[5:10]
    y_ref, so_ref, co_ref, ue_ref, ce_ref, acc_ref = refs[-6:]
    t = pl.program_id(1)
    width = ue_ref.shape[1]

    @pl.when(t == 0)
    def _():
        ue_ref[0:SHORT_PAD, :] = jnp.zeros((SHORT_PAD, width), F32)
        ce_ref[0:CONF_PAD, :] = jnp.zeros((CONF_PAD, width), F32)
        if has_state:
            ue_ref[SHORT_PAD - (SHORT_W - 1):SHORT_PAD, :] = ss_ref[0]
            ce_ref[CONF_PAD - (CONF_W - 1):CONF_PAD, :] = cs_ref[0]

    ue_ref[SHORT_PAD:SHORT_PAD + tt, :] = sc_ref[...] * sh_ref[...]
    ce_ref[CONF_PAD:CONF_PAD + tt, :] = cv_ref[...] * jax.nn.sigmoid(cg_ref[...])

    conv = jnp.zeros((tt, width), F32)
    for jt in range(SHORT_W):
        off = SHORT_PAD - (SHORT_W - 1) + jt
        conv = conv + sw_ref[jt:jt + 1, :] * ue_ref[off:off + tt, :]
    y_ref[:, :width] = (sb_ref[...] * conv).astype(y_ref.dtype)

    base = CONF_PAD - (CONF_W - 1)
    for r0 in range(0, tt, row_chunk):
        for c0 in range(0, width, LANES):
            acc = jnp.zeros((row_chunk, LANES), F32)
            for b in range(SUBLANES_F32):
                n_a = (CONF_W - 1 - b) // SUBLANES_F32 + 1
                rows = row_chunk + SUBLANES_F32 * (n_a - 1)
                win = ce_ref[r0 + base + b:r0 + base + b + rows, c0:c0 + LANES]
                for a in range(n_a):
                    jt = SUBLANES_F32 * a + b
                    acc = acc + cw_ref[jt:jt + 1, c0:c0 + LANES] * win[SUBLANES_F32 * a:SUBLANES_F32 * a + row_chunk, :]
            acc_ref[r0:r0 + row_chunk, c0:c0 + LANES] = acc
    y = acc_ref[...] + cb_ref[...]
    mu = jnp.mean(y, axis=-1, keepdims=True)
    var = jnp.mean(jnp.square(y - mu), axis=-1, keepdims=True)
    y = (y - mu) * lax.rsqrt(var + LN_EPS) * lg_ref[...] + lb_ref[...]
    y_ref[:, width:] = (y * jax.nn.sigmoid(y)).astype(y_ref.dtype)

    @pl.when(t == pl.num_programs(1) - 1)
    def _():
        so_ref[0] = ue_ref[SHORT_PAD + tt - (SHORT_W - 1):SHORT_PAD + tt, :]
        co_ref[0] = ce_ref[CONF_PAD + tt - (CONF_W - 1):CONF_PAD + tt, :]

    @pl.when(t < pl.num_programs(1) - 1)
    def _():
        ue_ref[0:SHORT_PAD, :] = ue_ref[tt:tt + SHORT_PAD, :]
        ce_ref[0:CONF_PAD, :] = ce_ref[tt:tt + CONF_PAD, :]


def _convs(z, row0, n_seq, seq, width, col_blk0, weights, states, mix, name):
    sw, cw, cb, lg, lb = weights
    tt = _pick_tile(seq, 256, SUBLANES_F32)
    nt = seq // tt
    rb0 = row0 // tt
    row_chunk = _pick_tile(tt, 64, SUBLANES_F32)
    assert nt == 1 or tt >= CONF_PAD
    has_state = states is not None

    def zspec(k):
        return pl.BlockSpec((tt, width), lambda b, t: (rb0 + b * nt + t, col_blk0 + k))

    def full(a):
        return pl.BlockSpec(a.shape, lambda b, t: (0,) * a.ndim)

    in_specs = [zspec(k) for k in range(5)]
    args = [z] * 5
    if has_state:
        in_specs += [pl.BlockSpec((1, SHORT_W - 1, width), lambda b, t: (b, 0, 0)),
                     pl.BlockSpec((1, CONF_W - 1, width), lambda b, t: (b, 0, 0))]
        args += list(states)
    in_specs += [full(a) for a in (sw, cw, cb, lg, lb)]
    args += [sw, cw, cb, lg, lb]
    if mix is None:
        y_shape = jax.ShapeDtypeStruct((n_seq * seq, 2 * width), F32)
        y_col_blk, aliases = 0, {}
    else:
        y_shape = jax.ShapeDtypeStruct(mix.shape, mix.dtype)
        y_col_blk, aliases = mix.shape[1] // (2 * width) - 1, {len(args): 0}
        in_specs.append(pl.BlockSpec(memory_space=pl.ANY))
        args.append(mix)
    return pl.pallas_call(
        functools.partial(_conv_body, tt=tt, has_state=has_state, row_chunk=row_chunk),
        grid=(n_seq, nt), in_specs=in_specs,
        out_specs=[pl.BlockSpec((tt, 2 * width), lambda b, t: (b * nt + t, y_col_blk)),
                   pl.BlockSpec((1, SHORT_W - 1, width), lambda b, t: (b, 0, 0)),
                   pl.BlockSpec((1, CONF_W - 1, width), lambda b, t: (b, 0, 0))],
        out_shape=[y_shape,
                   jax.ShapeDtypeStruct((n_seq, SHORT_W - 1, width), F32),
                   jax.ShapeDtypeStruct((n_seq, CONF_W - 1, width), F32)],
        scratch_shapes=[pltpu.VMEM((SHORT_PAD + tt, width), F32), pltpu.VMEM((CONF_PAD + tt, width), F32),
                        pltpu.VMEM((tt, width), F32)],
        input_output_aliases=aliases,
        compiler_params=_params("parallel", "arbitrary"), name=name)(*args)


def _rope_tables(positions):
    half = HEAD_DIM // 2
    inv_freq = jnp.power(ROPE_THETA, -jnp.arange(half, dtype=F32) * 2.0 / HEAD_DIM)
    ang = positions.astype(F32)[:, None] * inv_freq[None, :]
    cos, sin = jnp.cos(ang), jnp.sin(ang)
    return jnp.concatenate([cos, cos], axis=-1), jnp.concatenate([-sin, sin], axis=-1)


def kernel(x_prompt, x_sample, cache_k, cache_v, state_short, state_conf, page_table, norm_ffn1, ffn1_w_gate, ffn1_w_up, ffn1_w_down, norm_mix, w_in, q_norm, k_norm, lambda_q1, lambda_k1, lambda_q2, lambda_k2, attn_out_norm, short_w, conf_w, conf_b, conf_ln_g, conf_ln_b, w_out, norm_ffn2, ffn2_w_gate, ffn2_w_up, ffn2_w_down):
    batch, seq, d_model = x_prompt.shape
    dec_b, dec_s, _ = x_sample.shape
    depth, n_pool, page, n_heads, _ = cache_k.shape
    past_len = page_table.shape[1] * page
    qk_cols = n_heads * 2 * HEAD_DIM
    d_short = short_w.shape[2]
    m_prompt, m_sample = batch * seq, dec_b * dec_s
    width = n_heads * V_DIM

    cache_k2 = cache_k.reshape(depth, n_pool, page * n_heads, V_DIM)
    cache_v2 = cache_v.reshape(depth, n_pool, page * n_heads, V_DIM)

    positions = jnp.concatenate([jnp.tile(jnp.arange(seq), batch), jnp.tile(past_len + jnp.arange(dec_s), dec_b)])
    cos, sin = _rope_tables(positions)
    n_maps = 2 * n_heads
    eye = jnp.eye(2, dtype=F32)

    x = jnp.concatenate([x_prompt.reshape(m_prompt, d_model), x_sample.reshape(m_sample, d_model)], axis=0)
    row = lambda a, l: a[l].reshape(1, -1)

    outs = {k: [] for k in ("sp", "cp", "ks", "vs", "ss", "cs")}
    kv_prompt = None
    for l in range(depth):
        lam_init = 0.8 - 0.6 * math.exp(-0.3 * l)
        lam_params = jnp.stack([lambda_q1[l], lambda_k1[l], lambda_q2[l], lambda_k2[l]])
        g_out = row(attn_out_norm, l)
        conv_w = (short_w[l], conf_w[l], row(conf_b, l), row(conf_ln_g, l), row(conf_ln_b, l))

        act, wd_b = _ffn_up(_norm(x, row(norm_ffn1, l)), ffn1_w_gate, ffn1_w_up, ffn1_w_down, l)
        x = _proj_stream(act, wd_b, x, "swiglu_down")

        z = _proj_cast(_norm(x, row(norm_mix, l)), w_in, None, l, "w_in_proj")
        qk = _qk_prep(z, cos, sin, jnp.stack([q_norm[l], k_norm[l]]), qk_cols)

        mix = _attn_prompt(qk, z, lam_params, g_out, batch, seq, n_heads, lam_init, x.shape)

        q_s = qk[m_prompt:, :qk_cols].reshape(dec_b, dec_s, n_heads, 2, HEAD_DIM) * (HEAD_DIM ** -0.5)
        q_t = jnp.einsum("bthmd,mn->bmdhnt", q_s, eye).reshape(dec_b, 2 * HEAD_DIM, n_maps * dec_s).astype(BF16)
        k_s = qk[m_prompt:, qk_cols:].reshape(dec_b, dec_s * n_heads, V_DIM)
        v_s = z[m_prompt:, 2 * qk_cols:2 * qk_cols + width].reshape(dec_b, dec_s * n_heads, V_DIM)
        attn_s = _attn_sample(page_table, q_t, cache_k2, cache_v2, k_s, v_s,
                              lam_params, g_out, l, n_heads, dec_s, lam_init)

        conv_blk0 = (2 * qk_cols + width) // d_short
        mix, sp, cp = _convs(z, 0, batch, seq, d_short, conv_blk0, conv_w, None, mix, "convs_prompt")
        y_s, ss, cs = _convs(z, m_prompt, dec_b, dec_s, d_short, conv_blk0, conv_w,
                             (state_short[l], state_conf[l]), None, "convs_sample")
        mix_s = jnp.concatenate([attn_s.reshape(m_sample, width), y_s], axis=1).astype(BF16)
        mix = lax.dynamic_update_slice(mix, mix_s, (m_prompt, 0))

        x = _proj_cast(mix, w_out, x, l, "w_out_proj")
        act, wd_b = _ffn_up(_norm(x, row(norm_ffn2, l)), ffn2_w_gate, ffn2_w_up, ffn2_w_down, l)
        x = _proj_stream(act, wd_b, x, "swiglu_down")

        kv_prompt = _kv_out(qk, z, kv_prompt, l, depth, m_prompt, n_heads)
        outs["ks"].append(k_s.reshape(dec_b, dec_s, n_heads, V_DIM))
        outs["vs"].append(v_s.reshape(dec_b, dec_s, n_heads, V_DIM))
        outs["sp"].append(sp)
        outs["cp"].append(cp)
        outs["ss"].append(ss)
        outs["cs"].append(cs)

    st = {k: jnp.stack(v) for k, v in outs.items()}
    k_prompt, v_prompt = (a.reshape(depth, batch, seq, n_heads, V_DIM) for a in kv_prompt)
    return (x[:m_prompt].reshape(batch, seq, d_model), x[m_prompt:].reshape(dec_b, dec_s, d_model),
            k_prompt, v_prompt, st["sp"], st["cp"], st["ks"], st["vs"], st["ss"], st["cs"])
```
